```python
import math
import jax, jax.numpy as jnp
from jax import lax
import numpy as np

D_MODEL = 1024
BATCH = 2
SEQ = 8192
DEPTH = 4
DEC_BATCH = 32
DEC_SEQ = 8
PAST_LEN = 8192
PAGE_SIZE = 128

HEAD_DIM = 64
GLA_HEADS = 4
GLA_DK = 32
GLA_DV = 64
GLA_LORA = 16
GLA_TAU = 16.0
GLA_CHUNK = 16
RWKV_HEADS = 4
RWKV_W_LORA = 64
RWKV_A_LORA = 64
RWKV_G_LORA = 128
RWKV_LN_EPS = 64e-5
SB_HEADS = 8
SB_BLOCK = 128
D_FF = 2816
N_EXPERTS = 8
TOP_K = 2
D_EXPERT = 3584
EPS = 1e-6

GLA_QK = GLA_HEADS * GLA_DK
GLA_V = GLA_HEADS * GLA_DV
GLA_COLS = 2 * GLA_QK + GLA_V + GLA_LORA + GLA_V
RWKV_W = RWKV_HEADS * HEAD_DIM
RWKV_COLS = 3 * RWKV_W + RWKV_W_LORA + RWKV_A_LORA + RWKV_G_LORA
SB_W = SB_HEADS * HEAD_DIM
SB_COLS = 3 * SB_W
IN_COLS = GLA_COLS + RWKV_COLS + SB_COLS
MIX_WIDTH = GLA_V + RWKV_W + SB_W

kernel_name = 'hymba_gla_rwkv7_stickbreak_adaln_step'


def rms_norm(x, g, eps=EPS):
    xf = x.astype(jnp.float32)
    y = xf * lax.rsqrt(jnp.mean(xf * xf, axis=-1, keepdims=True) + eps)
    return (y * g).astype(x.dtype)


def head_group_norm(y, g, b, eps):
    mu = jnp.mean(y, axis=-1, keepdims=True)
    var = jnp.mean(jnp.square(y - mu), axis=-1, keepdims=True)
    return (y - mu) * lax.rsqrt(var + eps) * g + b


def gla_chunked(q, k, v, log_a, S0):
    B, T, H, K = q.shape
    V = v.shape[-1]
    C = math.gcd(T, GLA_CHUNK)
    N = T // C
    rs = lambda a: a.astype(jnp.float32).reshape(B, N, C, H, a.shape[-1])
    qc, kc, vc, gc = rs(q), rs(k), rs(v), rs(log_a)
    b = jnp.cumsum(gc, axis=2)
    b_last = b[:, :, -1]
    causal = jnp.tril(jnp.ones((C, C), dtype=bool))[None, None, :, :, None, None]
    diff = b[:, :, :, None] - b[:, :, None, :]
    decay = jnp.where(causal, jnp.exp(jnp.where(causal, diff, 0.0)), 0.0)
    A = jnp.einsum('bnthk,bnshk,bntshk->bnhts', qc, kc, decay)
    o_intra = jnp.einsum('bnhts,bnshv->bnthv', A, vc)
    q_in = qc * jnp.exp(b)
    k_out = kc * jnp.exp(b_last[:, :, None] - b)
    a_chunk = jnp.exp(b_last)

    def step(S, xs):
        qi, ki, vi, ai = xs
        o = jnp.einsum('bthk,bhkv->bthv', qi, S)
        S = S * ai[..., None] + jnp.einsum('bshk,bshv->bhkv', ki, vi)
        return S, o

    tm = lambda a: jnp.moveaxis(a, 1, 0)
    S, o_inter = lax.scan(step, S0.astype(jnp.float32), (tm(q_in), tm(k_out), tm(vc), tm(a_chunk)))
    o = o_intra + jnp.moveaxis(o_inter, 0, 1)
    return o.reshape(B, T, H, V), S


def gla_mixer(p, S0, w_a2, b_a, norm_g):
    B, T, _ = p.shape
    q, k, v, g_low, r = jnp.split(p, [GLA_QK, 2 * GLA_QK, 2 * GLA_QK + GLA_V, 2 * GLA_QK + GLA_V + GLA_LORA], axis=-1)
    q = q.reshape(B, T, GLA_HEADS, GLA_DK) * (GLA_DK ** -0.5)
    k = k.reshape(B, T, GLA_HEADS, GLA_DK)
    v = v.reshape(B, T, GLA_HEADS, GLA_DV)
    log_a = jax.nn.log_sigmoid((g_low @ w_a2 + b_a).astype(jnp.float32)) / GLA_TAU
    log_a = log_a.reshape(B, T, GLA_HEADS, GLA_DK)
    o, S = gla_chunked(q, k, v, log_a, S0)
    o = rms_norm(o, norm_g).reshape(B, T, GLA_V)
    return (o * jax.nn.silu(r.astype(jnp.float32))).astype(p.dtype), S


def rwkv_mixer(p, shift0, S0, mu, w0, w_w2, a0, w_a2, w_g2, k_k, k_a, r_k, ln_g, ln_b):
    B, T, _ = p.shape
    H, D = RWKV_HEADS, HEAD_DIM
    pf = p.astype(jnp.float32)
    p_prev = jnp.concatenate([shift0[:, None].astype(jnp.float32), pf[:, :-1]], axis=1)
    xm = pf + (p_prev - pf) * mu
    r, k, v, xw, xa, xg = jnp.split(xm, [RWKV_W, 2 * RWKV_W, 3 * RWKV_W, 3 * RWKV_W + RWKV_W_LORA, 3 * RWKV_W + RWKV_W_LORA + RWKV_A_LORA], axis=-1)
    w = -jax.nn.softplus(-(w0 + jnp.tanh(xw) @ w_w2)) - 0.5
    decay = jnp.exp(-jnp.exp(w))
    a = jax.nn.sigmoid(a0 + xa @ w_a2)
    g = jax.nn.sigmoid(xg) @ w_g2
    kk = (k * k_k).reshape(B, T, H, D)
    kk = kk / jnp.maximum(jnp.linalg.norm(kk, axis=-1, keepdims=True), 1e-12)
    k = k * (1.0 + (a - 1.0) * k_a)
    hd = lambda t: t.reshape(B, T, H, D)
    r, k, v, a, decay = hd(r), hd(k), hd(v), hd(a), hd(decay)

    def step(S, xs):
        rt, kt, vt, dt, kkt, at = xs
        sa = jnp.einsum('bhvk,bhk->bhv', S, -kkt)
        S = S * dt[:, :, None, :] + sa[..., None] * (kkt * at)[:, :, None, :] + vt[..., None] * kt[:, :, None, :]
        y = jnp.einsum('bhvk,bhk->bhv', S, rt)
        return S, y

    tm = lambda t: jnp.moveaxis(t, 1, 0)
    S, y = lax.scan(step, S0.astype(jnp.float32), (tm(r), tm(k), tm(v), tm(decay), tm(kk), tm(a)))
    y = jnp.moveaxis(y, 0, 1)
    y = head_group_norm(y, ln_g.reshape(H, D), ln_b.reshape(H, D), RWKV_LN_EPS)
    y = y + jnp.sum(r * k * r_k, axis=-1, keepdims=True) * v
    y = y.reshape(B, T, RWKV_W) * g
    return y.astype(p.dtype), S, p[:, -1]


def sb_block(q, k, v, q_pos, bias):
    z = jnp.einsum('bqhd,bkhd->bhqk', q, k) * (HEAD_DIM ** -0.5) + bias[None, :, None, None]
    mask = jnp.arange(k.shape[1], dtype=jnp.int32)[None, :] < q_pos[:, None]
    log_keep = jnp.where(mask, jax.nn.log_sigmoid(-z), 0.0)
    after = lax.cumsum(log_keep, axis=3, reverse=True) - log_keep
    A = jnp.where(mask, jnp.exp(jax.nn.log_sigmoid(z) + after), 0.0)
    return jnp.einsum('bhqk,bkhd->bqhd', A, v)


def sb_mixer(p, q_g, k_g, bias, k_past, v_past):
    B, T, _ = p.shape
    q, k, v = jnp.split(p, [SB_W, 2 * SB_W], axis=-1)
    q = rms_norm(q.reshape(B, T, SB_HEADS, HEAD_DIM), q_g)
    k = rms_norm(k.reshape(B, T, SB_HEADS, HEAD_DIM), k_g)
    v = v.reshape(B, T, SB_HEADS, HEAD_DIM)
    qf = q.astype(jnp.float32)
    bias = bias.astype(jnp.float32)
    if k_past is None:
        kf, vf = k.astype(jnp.float32), v.astype(jnp.float32)
        nb = T // SB_BLOCK
        qb = jnp.moveaxis(qf.reshape(B, nb, SB_BLOCK, SB_HEADS, HEAD_DIM), 1, 0)
        starts = jnp.arange(nb, dtype=jnp.int32) * SB_BLOCK
        ob = lax.map(lambda a: sb_block(a[0], kf, vf, a[1] + jnp.arange(SB_BLOCK, dtype=jnp.int32), bias), (qb, starts))
        o = jnp.moveaxis(ob, 0, 1)
    else:
        past = k_past.shape[1]
        kf = jnp.concatenate([k_past.astype(jnp.float32), k.astype(jnp.float32)], axis=1)
        vf = jnp.concatenate([v_past.astype(jnp.float32), v.astype(jnp.float32)], axis=1)
        o = sb_block(qf, kf, vf, past + jnp.arange(T, dtype=jnp.int32), bias)
    return o.reshape(B, T, SB_W).astype(p.dtype), k, v


def swiglu(h, w1, w3, w2):
    return (jax.nn.silu(h @ w1) * (h @ w3)) @ w2


def moe(h, router, w1, w3, w2):
    logits = (h @ router).astype(jnp.float32)
    top_v, top_i = lax.top_k(logits, TOP_K)
    gates = jax.nn.softmax(top_v, axis=-1)
    out = jnp.zeros_like(h)
    for e in range(N_EXPERTS):
        ge = jnp.sum(jnp.where(top_i == e, gates, 0.0), axis=-1, keepdims=True).astype(h.dtype)
        out = out + ge * swiglu(h, w1[e], w3[e], w2[e])
    return out


def layer(l, x, c, gla0, rwkv0, shift0, k_past, v_past, W):
    mod = jax.nn.silu(c) @ W['w_ada'][l] + W['b_ada'][l]
    sh1, sc1, g1, sh2, sc2, g2 = jnp.split(mod[:, None, :].astype(x.dtype), 6, axis=-1)
    h = rms_norm(x, W['norm1_g'][l]) * (1.0 + sc1) + sh1
    p = h @ W['w_in'][l]
    p_gla, p_rwkv, p_sb = jnp.split(p, [GLA_COLS, GLA_COLS + RWKV_COLS], axis=-1)
    o_gla, gla_S = gla_mixer(p_gla, gla0, W['gla_w_a2'][l], W['gla_b_a'][l], W['gla_norm_g'][l])
    o_rwkv, rwkv_S, shift_row = rwkv_mixer(p_rwkv, shift0, rwkv0, W['rwkv_mu'][l], W['rwkv_w0'][l], W['rwkv_w_w2'][l], W['rwkv_a0'][l], W['rwkv_w_a2'][l], W['rwkv_w_g2'][l], W['rwkv_k_k'][l], W['rwkv_k_a'][l], W['rwkv_r_k'][l], W['rwkv_ln_g'][l], W['rwkv_ln_b'][l])
    o_sb, k_rows, v_rows = sb_mixer(p_sb, W['sb_q_g'][l], W['sb_k_g'][l], W['sb_bias'][l], k_past, v_past)
    mix = jnp.concatenate([o_gla, o_rwkv, o_sb], axis=-1) @ W['w_out'][l]
    x = x + g1 * mix
    h2 = rms_norm(x, W['norm2_g'][l]) * (1.0 + sc2) + sh2
    if l % 2 == 0:
        f = swiglu(h2, W['ffn_w1'][l // 2], W['ffn_w3'][l // 2], W['ffn_w2'][l // 2])
    else:
        f = moe(h2, W['moe_router'][l // 2], W['moe_w1'][l // 2], W['moe_w3'][l // 2], W['moe_w2'][l // 2])
    x = x + g2 * f
    return x, (k_rows, v_rows, gla_S, rwkv_S, shift_row)


def setup_inputs(seed: int = 0) -> dict:
    key = jax.random.key(seed)
    ks = jax.random.split(key, 64)
    ctr = [0]

    def nxt():
        ctr[0] += 1
        return ks[ctr[0] - 1]

    def nrm(shape, scale=1.0):
        return jax.random.normal(nxt(), shape, jnp.float32) * scale

    def gain(shape):
        return 1.0 + nrm(shape, 0.05)

    n_pages = PAST_LEN // PAGE_SIZE
    n_pool = (5 * DEC_BATCH * n_pages) // 4
    n_dense = (DEPTH + 1) // 2
    n_moe = DEPTH // 2
    page_table = jax.random.permutation(nxt(), n_pool)[: DEC_BATCH * n_pages].reshape(DEC_BATCH, n_pages).astype(jnp.int32)
    inputs = {}
    inputs['x_prompt'] = nrm((BATCH, SEQ, D_MODEL))
    inputs['x_sample'] = nrm((DEC_BATCH, DEC_SEQ, D_MODEL))
    inputs['cache_sb_k'] = nrm((DEPTH, n_pool, PAGE_SIZE, SB_HEADS, HEAD_DIM))
    inputs['cache_sb_v'] = nrm((DEPTH, n_pool, PAGE_SIZE, SB_HEADS, HEAD_DIM))
    inputs['state_gla'] = nrm((DEPTH, DEC_BATCH, GLA_HEADS, GLA_DK, GLA_DV), 0.5)
    inputs['state_rwkv'] = nrm((DEPTH, DEC_BATCH, RWKV_HEADS, HEAD_DIM, HEAD_DIM), 0.3)
    inputs['state_shift'] = nrm((DEPTH, DEC_BATCH, RWKV_COLS))
    inputs['page_table'] = page_table
    inputs['c_prompt'] = nrm((BATCH, D_MODEL))
    inputs['c_sample'] = nrm((DEC_BATCH, D_MODEL))
    inputs['norm1_g'] = gain((DEPTH, D_MODEL))
    inputs['norm2_g'] = gain((DEPTH, D_MODEL))
    inputs['w_ada'] = nrm((DEPTH, D_MODEL, 6 * D_MODEL), 0.5 * D_MODEL ** -0.5)
    inputs['b_ada'] = nrm((DEPTH, 6 * D_MODEL), 0.02)
    inputs['w_in'] = nrm((DEPTH, D_MODEL, IN_COLS), D_MODEL ** -0.5)
    inputs['w_out'] = nrm((DEPTH, MIX_WIDTH, D_MODEL), MIX_WIDTH ** -0.5)
    inputs['gla_w_a2'] = nrm((DEPTH, GLA_LORA, GLA_QK), GLA_LORA ** -0.5)
    inputs['gla_b_a'] = 2.0 + nrm((DEPTH, GLA_QK), 0.5)
    inputs['gla_norm_g'] = gain((DEPTH, GLA_DV))
    inputs['rwkv_mu'] = jax.random.uniform(nxt(), (DEPTH, RWKV_COLS), jnp.float32)
    inputs['rwkv_w0'] = jax.random.uniform(nxt(), (DEPTH, RWKV_W), jnp.float32, minval=-6.0, maxval=-1.0)
    inputs['rwkv_w_w2'] = nrm((DEPTH, RWKV_W_LORA, RWKV_W), 0.5 * RWKV_W_LORA ** -0.5)
    inputs['rwkv_a0'] = nrm((DEPTH, RWKV_W), 0.5)
    inputs['rwkv_w_a2'] = nrm((DEPTH, RWKV_A_LORA, RWKV_W), RWKV_A_LORA ** -0.5)
    inputs['rwkv_w_g2'] = nrm((DEPTH, RWKV_G_LORA, RWKV_W), RWKV_G_LORA ** -0.5)
    inputs['rwkv_k_k'] = 0.85 + nrm((DEPTH, RWKV_W), 0.05)
    inputs['rwkv_k_a'] = 1.0 + nrm((DEPTH, RWKV_W), 0.05)
    inputs['rwkv_r_k'] = nrm((DEPTH, RWKV_HEADS, HEAD_DIM), 0.2)
    inputs['rwkv_ln_g'] = gain((DEPTH, RWKV_W))
    inputs['rwkv_ln_b'] = nrm((DEPTH, RWKV_W), 0.02)
    inputs['sb_q_g'] = gain((DEPTH, HEAD_DIM))
    inputs['sb_k_g'] = gain((DEPTH, HEAD_DIM))
    inputs['sb_bias'] = jnp.linspace(-1.0, -10.0, SB_HEADS, dtype=jnp.float32)[None, :] + nrm((DEPTH, SB_HEADS), 0.1)
    inputs['ffn_w1'] = nrm((n_dense, D_MODEL, D_FF), D_MODEL ** -0.5)
    inputs['ffn_w3'] = nrm((n_dense, D_MODEL, D_FF), D_MODEL ** -0.5)
    inputs['ffn_w2'] = nrm((n_dense, D_FF, D_MODEL), D_FF ** -0.5)
    inputs['moe_router'] = nrm((n_moe, D_MODEL, N_EXPERTS), D_MODEL ** -0.5)
    inputs['moe_w1'] = nrm((n_moe, N_EXPERTS, D_MODEL, D_EXPERT), D_MODEL ** -0.5)
    inputs['moe_w3'] = nrm((n_moe, N_EXPERTS, D_MODEL, D_EXPERT), D_MODEL ** -0.5)
    inputs['moe_w2'] = nrm((n_moe, N_EXPERTS, D_EXPERT, D_MODEL), D_EXPERT ** -0.5)
    return inputs


def reference(x_prompt, x_sample, cache_sb_k, cache_sb_v, state_gla, state_rwkv, state_shift, page_table, c_prompt, c_sample, norm1_g, norm2_g, w_ada, b_ada, w_in, w_out, gla_w_a2, gla_b_a, gla_norm_g, rwkv_mu, rwkv_w0, rwkv_w_w2, rwkv_a0, rwkv_w_a2, rwkv_w_g2, rwkv_k_k, rwkv_k_a, rwkv_r_k, rwkv_ln_g, rwkv_ln_b, sb_q_g, sb_k_g, sb_bias, ffn_w1, ffn_w3, ffn_w2, moe_router, moe_w1, moe_w3, moe_w2):
    W = dict(norm1_g=norm1_g, norm2_g=norm2_g, w_ada=w_ada, b_ada=b_ada, w_in=w_in, w_out=w_out,
             gla_w_a2=gla_w_a2, gla_b_a=gla_b_a, gla_norm_g=gla_norm_g,
             rwkv_mu=rwkv_mu, rwkv_w0=rwkv_w0, rwkv_w_w2=rwkv_w_w2, rwkv_a0=rwkv_a0, rwkv_w_a2=rwkv_w_a2,
             rwkv_w_g2=rwkv_w_g2, rwkv_k_k=rwkv_k_k, rwkv_k_a=rwkv_k_a, rwkv_r_k=rwkv_r_k,
             rwkv_ln_g=rwkv_ln_g, rwkv_ln_b=rwkv_ln_b, sb_q_g=sb_q_g, sb_k_g=sb_k_g, sb_bias=sb_bias,
             ffn_w1=ffn_w1, ffn_w3=ffn_w3, ffn_w2=ffn_w2,
             moe_router=moe_router, moe_w1=moe_w1, moe_w3=moe_w3, moe_w2=moe_w2)
    B = x_prompt.shape[0]
    DB = x_sample.shape[0]
    xp, xs = x_prompt, x_sample
    st_p, st_s = [], []
    for l in range(DEPTH):
        xp, sp = layer(l, xp, c_prompt,
                       jnp.zeros((B, GLA_HEADS, GLA_DK, GLA_DV), jnp.float32),
                       jnp.zeros((B, RWKV_HEADS, HEAD_DIM, HEAD_DIM), jnp.float32),
                       jnp.zeros((B, RWKV_COLS), x_prompt.dtype), None, None, W)
        k_past = cache_sb_k[l][page_table].reshape(DB, -1, SB_HEADS, HEAD_DIM)
        v_past = cache_sb_v[l][page_table].reshape(DB, -1, SB_HEADS, HEAD_DIM)
        xs, ss = layer(l, xs, c_sample, state_gla[l], state_rwkv[l], state_shift[l], k_past, v_past, W)
        st_p.append(sp)
        st_s.append(ss)
    new_k_prompt = jnp.stack([s[0] for s in st_p])
    new_v_prompt = jnp.stack([s[1] for s in st_p])
    new_gla_prompt = jnp.stack([s[2] for s in st_p])
    new_rwkv_prompt = jnp.stack([s[3] for s in st_p])
    new_shift_prompt = jnp.stack([s[4] for s in st_p])
    new_k_sample = jnp.stack([s[0] for s in st_s])
    new_v_sample = jnp.stack([s[1] for s in st_s])
    new_gla_sample = jnp.stack([s[2] for s in st_s])
    new_rwkv_sample = jnp.stack([s[3] for s in st_s])
    new_shift_sample = jnp.stack([s[4] for s in st_s])
    return (xp, xs, new_k_prompt, new_v_prompt, new_gla_prompt, new_rwkv_prompt, new_shift_prompt, new_k_sample, new_v_sample, new_gla_sample, new_rwkv_sample, new_shift_sample)
```

```python
import functools
import math

import numpy as np
import jax
import jax.numpy as jnp
from jax import lax
from jax.experimental import pallas as pl
from jax.experimental.pallas import tpu as pltpu

F32 = jnp.float32
BF16 = jnp.bfloat16
HIGHEST = lax.Precision.HIGHEST

HEAD_DIM = 64
GLA_HEADS = 4
GLA_DK = 32
GLA_DV = 64
GLA_LORA = 16
GLA_TAU = 16.0
GLA_CHUNK = 16
RWKV_HEADS = 4
RWKV_W_LORA = 64
RWKV_A_LORA = 64
RWKV_G_LORA = 128
RWKV_LN_EPS = 64e-5
SB_HEADS = 8
TOP_K = 2
EPS = 1e-6
LOG2E = math.log2(math.e)

GLA_QK = GLA_HEADS * GLA_DK
GLA_V = GLA_HEADS * GLA_DV
GLA_PCOLS = 2 * GLA_QK + 2 * GLA_V + 128
RWKV_W = RWKV_HEADS * HEAD_DIM
RWKV_COLS = 3 * RWKV_W + RWKV_W_LORA + RWKV_A_LORA + RWKV_G_LORA
SB_W = SB_HEADS * HEAD_DIM
SB_COLS = 3 * SB_W
LANES = 128
PAGE = 128
VMEM_LIMIT = 56 * 1024 * 1024


def _cparams(sem):
    return pltpu.CompilerParams(dimension_semantics=sem, vmem_limit_bytes=VMEM_LIMIT)


def _tile(n, pref):
    if n <= pref:
        return n
    t = pref
    while n % t:
        t -= 8
    return t


def _block_ones(n, group, value=1.0):
    idx = np.arange(n) // group
    return jnp.asarray((idx[:, None] == idx[None, :]).astype(np.float32) * value)


def _dot(a, b):
    return jnp.dot(a, b, preferred_element_type=F32)


def _dot_hi(a, b):
    return jnp.dot(a, b, preferred_element_type=F32, precision=HIGHEST)


def _dot_nt(a, b, precision=None):
    return lax.dot_general(a, b, (((1,), (1,)), ((), ())), preferred_element_type=F32, precision=precision)


def _dot_tn(a, b, precision=None):
    return lax.dot_general(a, b, (((0,), (0,)), ((), ())), preferred_element_type=F32, precision=precision)


def _dot_split(x, m_bf16):
    hi = x.astype(BF16)
    lo = (x - hi.astype(F32)).astype(BF16)
    return _dot(hi, m_bf16) + _dot(lo, m_bf16)


def _sigmoid(x):
    return 1.0 / (1.0 + jnp.exp(-x))


def _softplus(x):
    return jnp.maximum(x, 0.0) + jnp.log(1.0 + jnp.exp(-jnp.abs(x)))


def _mod_kernel(c_ref, w_ref, b_ref, o_ref):
    c = c_ref[...]
    o_ref[0] = _dot_hi(c * _sigmoid(c), w_ref[0]) + b_ref[0]


def _modulation(c_all, w_ada, b_ada):
    depth, d, d6 = w_ada.shape
    r = c_all.shape[0]
    tn = 1024
    return pl.pallas_call(
        _mod_kernel,
        grid=(depth, d6 // tn),
        in_specs=[
            pl.BlockSpec((r, d), lambda l, j: (0, 0)),
            pl.BlockSpec((1, d, tn), lambda l, j: (l, 0, j)),
            pl.BlockSpec((1, 1, tn), lambda l, j: (l, 0, j)),
        ],
        out_specs=pl.BlockSpec((1, r, tn), lambda l, j: (l, 0, j)),
        out_shape=jax.ShapeDtypeStruct((depth, r, d6), F32),
        compiler_params=_cparams(("arbitrary", "arbitrary")),
    )(c_all, w_ada, b_ada.reshape(depth, 1, d6))


def _mod_spec(mod, tm, tpb, chunk):
    d = mod.shape[2] // 6
    if mod.shape[1] == 1:
        return pl.BlockSpec((1, 1, d), lambda i, *_: (i // tpb, 0, chunk))
    return pl.BlockSpec((1, tm, d), lambda i, *_: (i // tpb, i % tpb, chunk))


def _inproj_kernel(x_ref, sh_ref, sc_ref, g_ref, w_ref, pg_ref, pr_ref, ps_ref):
    x = x_ref[...]
    ms = jnp.mean(x * x, axis=-1, keepdims=True)
    h = x * lax.rsqrt(ms + EPS) * g_ref[...]
    h = h * (1.0 + sc_ref[0]) + sh_ref[0]
    p = _dot(h.astype(BF16), w_ref[...])
    pg_ref[...] = p[:, :GLA_PCOLS]
    pr_ref[...] = p[:, GLA_PCOLS:GLA_PCOLS + RWKV_COLS]
    ps_ref[...] = p[:, GLA_PCOLS + RWKV_COLS:]


def _inproj(x, mod, norm_g, w):
    n, d = x.shape
    nb = mod.shape[0]
    tm = _tile(n // nb, 256)
    tpb = (n // nb) // tm
    ncols = w.shape[1]
    return pl.pallas_call(
        _inproj_kernel,
        grid=(n // tm,),
        in_specs=[
            pl.BlockSpec((tm, d), lambda i: (i, 0)),
            _mod_spec(mod, tm, tpb, 0),
            _mod_spec(mod, tm, tpb, 1),
            pl.BlockSpec((1, d), lambda i: (0, 0)),
            pl.BlockSpec((d, ncols), lambda i: (0, 0)),
        ],
        out_specs=[
            pl.BlockSpec((tm, GLA_PCOLS), lambda i: (i, 0)),
            pl.BlockSpec((tm, RWKV_COLS), lambda i: (i, 0)),
            pl.BlockSpec((tm, SB_COLS), lambda i: (i, 0)),
        ],
        out_shape=[
            jax.ShapeDtypeStruct((n, GLA_PCOLS), F32),
            jax.ShapeDtypeStruct((n, RWKV_COLS), F32),
            jax.ShapeDtypeStruct((n, SB_COLS), F32),
        ],
        compiler_params=_cparams(("arbitrary",)),
    )(x, mod, mod, norm_g.reshape(1, d), w)


def _gla_kernel(p_ref, s0_ref, wa2_ref, ba_ref, ng_ref, hexp_ref, havg_ref, bd_ref, tril_ref,
                o_ref, s_ref, la_scr, *, chunk, n_chunks):
    c_len = chunk

    @pl.when(pl.program_id(1) == 0)
    def _():
        s_ref[0] = s0_ref[0]

    glow = p_ref[0, :, 2 * GLA_QK + 2 * GLA_V:]
    x = _dot_hi(glow, wa2_ref[...]) + ba_ref[...]
    la_scr[...] = -_softplus(-x) * (1.0 / GLA_TAU)
    row = lax.broadcasted_iota(jnp.int32, (c_len, 1), 0)
    hexp = hexp_ref[...]
    havg = havg_ref[...]
    bd = bd_ref[...]
    tril = tril_ref[...]
    ng = ng_ref[...]

    def body(c, carry):
        sl = pl.ds(pl.multiple_of(c * c_len, c_len), c_len)
        q = p_ref[0, sl, 0:GLA_QK] * (GLA_DK ** -0.5)
        k = p_ref[0, sl, GLA_QK:2 * GLA_QK]
        v = p_ref[0, sl, 2 * GLA_QK:2 * GLA_QK + GLA_V]
        r = p_ref[0, sl, 2 * GLA_QK + GLA_V:2 * GLA_QK + 2 * GLA_V]
        b = _dot_hi(tril, la_scr[sl, :])
        b_last = b[c_len - 1:c_len, :]
        zs = []
        for s in range(c_len):
            m = row >= s
            e = jnp.exp(jnp.where(m, b - b[s:s + 1, :], 0.0))
            zs.append(jnp.where(m, q * k[s:s + 1, :] * e, 0.0))
        a_exp = _dot_hi(jnp.concatenate(zs, axis=0), hexp)
        o = jnp.zeros((c_len, GLA_V), F32)
        for s in range(c_len):
            o = o + a_exp[s * c_len:(s + 1) * c_len, :] * v[s:s + 1, :]
        st = s_ref[0]
        o = o + _dot_nt(q * jnp.exp(b), st, HIGHEST)
        k_out = k * jnp.exp(b_last - b)
        s_ref[0] = st * jnp.exp(b_last) + _dot_tn(v, k_out, HIGHEST) * bd
        ms = _dot_hi(o * o, havg)
        on = o * lax.rsqrt(ms + EPS) * ng
        o_ref[0, sl, :] = (on * (r * _sigmoid(r))).astype(o_ref.dtype)
        return carry

    lax.fori_loop(0, n_chunks, body, 0)


def _gla(p_gla, s0t, wa2p, ba, ng_t, batch, chunk):
    n = p_gla.shape[0]
    t = n // batch
    tt = _tile(t, 512)
    n_chunks = tt // chunk
    p3 = p_gla.reshape(batch, t, GLA_PCOLS)
    hexp = jnp.asarray((np.arange(GLA_QK)[:, None] // GLA_DK == np.arange(GLA_V)[None, :] // GLA_DV).astype(np.float32))
    havg = _block_ones(GLA_V, GLA_DV, 1.0 / GLA_DV)
    bd = jnp.asarray((np.arange(GLA_V)[:, None] // GLA_DV == np.arange(GLA_QK)[None, :] // GLA_DK).astype(np.float32))
    tril = jnp.asarray(np.tril(np.ones((chunk, chunk), np.float32)))
    const = lambda shape: pl.BlockSpec(shape, lambda b, i: (0,) * len(shape))
    o, s = pl.pallas_call(
        functools.partial(_gla_kernel, chunk=chunk, n_chunks=n_chunks),
        grid=(batch, t // tt),
        in_specs=[
            pl.BlockSpec((1, tt, GLA_PCOLS), lambda b, i: (b, i, 0)),
            pl.BlockSpec((1, GLA_V, GLA_QK), lambda b, i: (b, 0, 0)),
            const((LANES, GLA_QK)), const((1, GLA_QK)), const((1, GLA_V)),
            const((GLA_QK, GLA_V)), const((GLA_V, GLA_V)), const((GLA_V, GLA_QK)), const((chunk, chunk)),
        ],
        out_specs=[
            pl.BlockSpec((1, tt, GLA_V), lambda b, i: (b, i, 0)),
            pl.BlockSpec((1, GLA_V, GLA_QK), lambda b, i: (b, 0, 0)),
        ],
        out_shape=[
            jax.ShapeDtypeStruct((batch, t, GLA_V), BF16),
            jax.ShapeDtypeStruct((batch, GLA_V, GLA_QK), F32),
        ],
        scratch_shapes=[pltpu.VMEM((tt, GLA_QK), F32)],
        compiler_params=_cparams(("arbitrary", "arbitrary")),
    )(p3, s0t, wa2p, ba, ng_t, hexp, havg, bd, tril)
    return o.reshape(n, GLA_V), s


def _gla_state_to_bd(s0):
    b = s0.shape[0]
    eye = jnp.eye(GLA_HEADS, dtype=s0.dtype)
    st = jnp.einsum('bhkv,hg->bhvgk', s0, eye)
    return st.reshape(b, GLA_V, GLA_QK)


def _gla_state_from_bd(st):
    b = st.shape[0]
    s5 = st.reshape(b, GLA_HEADS, GLA_DV, GLA_HEADS, GLA_DK)
    diag = jnp.stack([s5[:, h, :, h, :] for h in range(GLA_HEADS)], axis=1)
    return jnp.swapaxes(diag, 2, 3)


def _rwkv_kernel(p_ref, sh0_ref, s0_ref, mu_ref, w0_ref, ww2_ref, a0_ref, wa2_ref, wg2_ref, kk_ref, ka_ref,
                 rk_ref, lng_ref, lnb_ref, ones_ref, onesb_ref, avg_ref, sel_ref,
                 o_ref, s_ref, prev_scr, nkk_scr, d_scr, b_scr, k_scr, r_scr, v_scr, y_scr, *, nbb, tt):
    first = pl.program_id(1) == 0

    @pl.when(first)
    def _():
        s_ref[...] = s0_ref[...]
        prev_scr[...] = sh0_ref[...]

    ones_f = ones_ref[...]
    ones_b = onesb_ref[...]
    avg = avg_ref[...]
    sel = sel_ref[...]
    row = lax.broadcasted_iota(jnp.int32, (tt, 1), 0)
    gate = []
    for bb in range(nbb):
        p = p_ref[bb]
        p_prev = jnp.where(row == 0, prev_scr[bb], pltpu.roll(p, 1, axis=0))
        prev_scr[bb] = p[tt - 1:tt, :]
        xm = p + (p_prev - p) * mu_ref[...]
        r = xm[:, 0:RWKV_W]
        k = xm[:, RWKV_W:2 * RWKV_W]
        v = xm[:, 2 * RWKV_W:3 * RWKV_W]
        xwa = xm[:, 3 * RWKV_W:3 * RWKV_W + LANES]
        xg = xm[:, 3 * RWKV_W + LANES:]
        w = -_softplus(-(w0_ref[...] + _dot_hi(jnp.tanh(xwa), ww2_ref[...]))) - 0.5
        a = _sigmoid(a0_ref[...] + _dot_hi(xwa, wa2_ref[...]))
        gate.append(_dot_hi(_sigmoid(xg), wg2_ref[...]))
        kk = k * kk_ref[...]
        nrm = jnp.sqrt(_dot_hi(kk * kk, ones_f))
        kk = kk / jnp.maximum(nrm, 1e-12)
        k2 = k * (1.0 + (a - 1.0) * ka_ref[...])
        nkk_scr[bb] = -kk
        d_scr[bb] = jnp.exp(-jnp.exp(w))
        b_scr[bb] = kk * a
        k_scr[bb] = k2
        r_scr[bb] = r
        v_scr[bb] = v

    def read_out(s, bb, ts):
        yb = _dot_split(s * jnp.broadcast_to(r_scr[bb, ts, :], (HEAD_DIM, RWKV_W)), ones_b)
        y_scr[bb, ts, :] = jnp.sum(yb * sel, axis=0, keepdims=True)

    def step(t, states):
        new = []
        for bb in range(nbb):
            s = states[bb]
            ts = pl.ds(t, 1)
            tp = pl.ds(jnp.maximum(t - 1, 0), 1)
            bc = lambda ref, rows: jnp.broadcast_to(ref[bb, rows, :], (HEAD_DIM, RWKV_W))
            lhs = jnp.concatenate([s * bc(nkk_scr, ts), s * bc(r_scr, tp), sel * bc(v_scr, ts)], axis=0)
            res = _dot_split(lhs, ones_b)
            sa = res[0:HEAD_DIM]
            y_scr[bb, tp, :] = jnp.sum(res[HEAD_DIM:2 * HEAD_DIM] * sel, axis=0, keepdims=True)
            vb = res[2 * HEAD_DIM:]
            new.append(s * bc(d_scr, ts) + sa * bc(b_scr, ts) + vb * bc(k_scr, ts))
        return tuple(new)

    states = lax.fori_loop(0, tt, step, tuple(s_ref[bb] for bb in range(nbb)))
    for bb in range(nbb):
        s_ref[bb] = states[bb]
        read_out(states[bb], bb, pl.ds(tt - 1, 1))
        y = y_scr[bb]
        mean = _dot_hi(y, avg)
        yc = y - mean
        var = _dot_hi(yc * yc, avg)
        yn = yc * lax.rsqrt(var + RWKV_LN_EPS) * lng_ref[...] + lnb_ref[...]
        bonus = _dot_hi(r_scr[bb] * k_scr[bb] * rk_ref[...], ones_f) * v_scr[bb]
        o_ref[bb] = ((yn + bonus) * gate[bb]).astype(o_ref.dtype)


def _rwkv(p_rwkv, shift0, s0, wl, batch, nbb):
    n = p_rwkv.shape[0]
    t = n // batch
    tt = _tile(t, 256)
    p3 = p_rwkv.reshape(batch, t, RWKV_COLS)
    ones_f = _block_ones(RWKV_W, HEAD_DIM)
    avg = _block_ones(RWKV_W, HEAD_DIM, 1.0 / HEAD_DIM)
    sel = jnp.asarray((np.arange(HEAD_DIM)[:, None] == np.arange(RWKV_W)[None, :] % HEAD_DIM).astype(np.float32))
    const = lambda shape: pl.BlockSpec(shape, lambda b, i: (0,) * len(shape))
    row = const((1, RWKV_W))
    scr = lambda: pltpu.VMEM((nbb, tt, RWKV_W), F32)
    o, s = pl.pallas_call(
        functools.partial(_rwkv_kernel, nbb=nbb, tt=tt),
        grid=(batch // nbb, t // tt),
        in_specs=[
            pl.BlockSpec((nbb, tt, RWKV_COLS), lambda b, i: (b, i, 0)),
            pl.BlockSpec((nbb, 1, RWKV_COLS), lambda b, i: (b, 0, 0)),
            pl.BlockSpec((nbb, HEAD_DIM, RWKV_W), lambda b, i: (b, 0, 0)),
            const((1, RWKV_COLS)), row, const((LANES, RWKV_W)), row, const((LANES, RWKV_W)),
            const((RWKV_G_LORA, RWKV_W)), row, row, row, row, row,
            const((RWKV_W, RWKV_W)), const((RWKV_W, RWKV_W)), const((RWKV_W, RWKV_W)), const((HEAD_DIM, RWKV_W)),
        ],
        out_specs=[
            pl.BlockSpec((nbb, tt, RWKV_W), lambda b, i: (b, i, 0)),
            pl.BlockSpec((nbb, HEAD_DIM, RWKV_W), lambda b, i: (b, 0, 0)),
        ],
        out_shape=[
            jax.ShapeDtypeStruct((batch, t, RWKV_W), BF16),
            jax.ShapeDtypeStruct((batch, HEAD_DIM, RWKV_W), F32),
        ],
        scratch_shapes=[pltpu.VMEM((nbb, 1, RWKV_COLS), F32)] + [scr() for _ in range(7)],
        compiler_params=_cparams(("arbitrary", "arbitrary")),
    )(p3, shift0, s0, wl['mu'], wl['w0'], wl['ww2'], wl['a0'], wl['wa2'], wl['wg2'], wl['kk'], wl['ka'],
      wl['rk'], wl['lng'], wl['lnb'], ones_f, ones_f.astype(BF16), avg, sel)
    return o.reshape(n, RWKV_W), s


def _rwkv_state_in(s0):
    b = s0.shape[0]
    return jnp.transpose(s0, (0, 2, 1, 3)).reshape(b, HEAD_DIM, RWKV_W)


def _rwkv_state_out(s):
    b = s.shape[0]
    return jnp.transpose(s.reshape(b, HEAD_DIM, RWKV_HEADS, HEAD_DIM), (0, 2, 1, 3))


def _sbprep_kernel(p_ref, qg_ref, kg_ref, avg_ref, *rest, aug):
    if aug:
        qrow_ref, krow_ref, q_ref, k_ref, kb_ref, vb_ref = rest
    else:
        q_ref, k_ref, kb_ref, vb_ref = rest
    avg = avg_ref[...]
    q = p_ref[:, 0:SB_W]
    k = p_ref[:, SB_W:2 * SB_W]
    qn = q * lax.rsqrt(_dot_hi(q * q, avg) + EPS) * (qg_ref[...] * (HEAD_DIM ** -0.5 * LOG2E))
    kn = k * lax.rsqrt(_dot_hi(k * k, avg) + EPS) * kg_ref[...]
    k_ref[...] = kn
    vb_ref[...] = p_ref[:, 2 * SB_W:].astype(BF16)
    if aug:
        lo = lax.broadcasted_iota(jnp.int32, (1, LANES), 1) < HEAD_DIM

        def widen(x, row):
            parts = []
            for h in range(SB_HEADS):
                blk = x[:, (h // 2) * LANES:(h // 2 + 1) * LANES]
                if h % 2:
                    blk = pltpu.roll(blk, HEAD_DIM, axis=1)
                parts.append(jnp.where(lo, blk, 0.0))
            return (jnp.concatenate(parts, axis=1) + row).astype(BF16)

        q_ref[...] = widen(qn, qrow_ref[...])
        kb_ref[...] = widen(kn, krow_ref[...])
    else:
        q_ref[...] = qn.astype(BF16)
        kb_ref[...] = kn.astype(BF16)


def _sbprep(p_sb, qg_t, kg_t, bias=None):
    n = p_sb.shape[0]
    tm = _tile(n, 512)
    aug = bias is not None
    avg = _block_ones(SB_W, HEAD_DIM, 1.0 / HEAD_DIM)
    wide = SB_HEADS * LANES if aug else SB_W
    out = lambda w: pl.BlockSpec((tm, w), lambda i: (i, 0))
    const = lambda shape: pl.BlockSpec(shape, lambda i: (0,) * len(shape))
    in_specs = [pl.BlockSpec((tm, SB_COLS), lambda i: (i, 0)), const((1, SB_W)), const((1, SB_W)), const((SB_W, SB_W))]
    args = [p_sb, qg_t, kg_t, avg]
    if aug:
        b2 = bias * LOG2E
        hi = b2.astype(BF16).astype(F32)
        lo = (b2 - hi).astype(BF16).astype(F32)
        qrow = jnp.zeros((SB_HEADS, LANES), F32).at[:, HEAD_DIM].set(hi).at[:, HEAD_DIM + 1].set(lo)
        krow = jnp.zeros((SB_HEADS, LANES), F32).at[:, HEAD_DIM:HEAD_DIM + 2].set(1.0)
        in_specs += [const((1, wide)), const((1, wide))]
        args += [qrow.reshape(1, wide), krow.reshape(1, wide)]
    return pl.pallas_call(
        functools.partial(_sbprep_kernel, aug=aug),
        grid=(n // tm,),
        in_specs=in_specs,
        out_specs=[out(wide), out(SB_W), out(wide), out(SB_W)],
        out_shape=[
            jax.ShapeDtypeStruct((n, wide), BF16),
            jax.ShapeDtypeStruct((n, SB_W), F32),
            jax.ShapeDtypeStruct((n, wide), BF16),
            jax.ShapeDtypeStruct((n, SB_W), BF16),
        ],
        compiler_params=_cparams(("arbitrary",)),
    )(*args)


def _sb_logits_stage(z, mask, cum):
    keep = jnp.maximum(z, 0.0) + jnp.log2(1.0 + jnp.exp2(-jnp.abs(z)))
    if mask is not None:
        keep = jnp.where(mask, keep, 0.0)
    tk = z.shape[1]
    cs = _dot(keep.astype(BF16), cum)
    return z - keep - cs[:, :tk], cs[:, tk:]


def _sb_weights(part, r_run, mask):
    reps = part.shape[1] // r_run.shape[1]
    e = jnp.exp2(part - (jnp.concatenate([r_run] * reps, axis=1) if reps > 1 else r_run))
    if mask is not None:
        e = jnp.where(mask, e, 0.0)
    return e


def _sb_prompt_kernel(q_ref, k_ref, v_ref, cum_ref, o_ref, part_scr, tot_scr, *, tq):
    i = pl.program_id(2)
    cum = cum_ref[...]
    rows = lax.broadcasted_iota(jnp.int32, (tq, tq), 0)
    cols = lax.broadcasted_iota(jnp.int32, (tq, tq), 1)
    diag = cols < rows

    def stage1(j, mask):
        ks = pl.ds(pl.multiple_of(j * tq, tq), tq)
        return [_sb_logits_stage(_dot_nt(q_ref[:, u * LANES:(u + 1) * LANES], k_ref[ks, u * LANES:(u + 1) * LANES]),
                                 mask, cum) for u in range(2)]

    def stage2(j, pend, carry, mask, valid):
        vb = v_ref[pl.ds(pl.multiple_of(j * tq, tq), tq), :]
        out = []
        for u in range(2):
            part, tot = pend[u]
            acc, r_run = carry[u]
            pv = _dot(_sb_weights(part, r_run, mask).astype(BF16), vb)
            if valid is not None:
                pv = jnp.where(valid, pv, 0.0)
                tot = jnp.where(valid, tot, 0.0)
            out.append((acc + pv, r_run + tot))
        return tuple(out)

    carry = tuple((jnp.zeros((tq, LANES), F32), jnp.zeros((tq, LANES), F32)) for _ in range(2))
    carry = stage2(i, stage1(i, diag), carry, diag, None)
    part_scr[...] = jnp.zeros_like(part_scr)
    tot_scr[...] = jnp.zeros_like(tot_scr)

    def body(n, carry):
        pend = [(part_scr[u], tot_scr[u]) for u in range(2)]
        carry = stage2(jnp.maximum(i - n, 0), pend, carry, None, n >= 1)
        nxt = stage1(jnp.maximum(i - 1 - n, 0), None)
        for u in range(2):
            part_scr[u] = nxt[u][0]
            tot_scr[u] = nxt[u][1]
        return carry

    carry = lax.fori_loop(0, i + 1, body, carry)
    lo = lax.broadcasted_iota(jnp.int32, (1, LANES), 1) < HEAD_DIM
    o_ref[...] = jnp.where(lo, carry[0][0], carry[1][0]).astype(o_ref.dtype)


def _cum_matrix(tk):
    j = np.arange(tk)
    later = (j[:, None] > j[None, :]).astype(np.float32)
    return jnp.asarray(np.concatenate([later, np.ones((tk, LANES), np.float32)], axis=1)).astype(BF16)


def _sb_prompt(q_aug, k_aug, vb, batch):
    n = vb.shape[0]
    t = n // batch
    tq = _tile(t, 256)
    tpb = t // tq
    return pl.pallas_call(
        functools.partial(_sb_prompt_kernel, tq=tq),
        grid=(batch, SB_HEADS // 2, tpb),
        in_specs=[
            pl.BlockSpec((tq, 2 * LANES), lambda b, h, i: (b * tpb + i, h)),
            pl.BlockSpec((t, 2 * LANES), lambda b, h, i: (b, h)),
            pl.BlockSpec((t, LANES), lambda b, h, i: (b, h)),
            pl.BlockSpec((tq, tq + LANES), lambda b, h, i: (0, 0)),
        ],
        out_specs=pl.BlockSpec((tq, LANES), lambda b, h, i: (b * tpb + i, h)),
        out_shape=jax.ShapeDtypeStruct((n, SB_W), BF16),
        scratch_shapes=[pltpu.VMEM((2, tq, tq), F32), pltpu.VMEM((2, tq, LANES), F32)],
        compiler_params=_cparams(("arbitrary", "arbitrary", "arbitrary")),
    )(q_aug, k_aug, vb, _cum_matrix(tq))


def _sb_sample_kernel(pt_ref, q_ref, kn_ref, vn_ref, bias_ref, cum_ref, *rest, n_pages, ppb, dec):
    k_refs = rest[:ppb]
    v_refs = rest[ppb:2 * ppb]
    o_ref, qh_scr, acc_scr, run_scr = rest[2 * ppb:]
    g = pl.program_id(1)
    cum = cum_ref[...]
    bias = bias_ref[...]

    def head_rows(ref, h):
        return ref[pl.ds(h, PAGE, stride=SB_HEADS), :].astype(BF16)

    def attend(k_ref, v_ref, mask):
        z = jnp.concatenate([_dot_nt(qh_scr[h].astype(BF16), head_rows(k_ref, h)) for h in range(SB_HEADS)], axis=0)
        part, tot = _sb_logits_stage(z + bias, mask, cum)
        e = _sb_weights(part, run_scr[...], mask)
        for h in range(SB_HEADS):
            acc_scr[h] += _dot(e[h * dec:(h + 1) * dec].astype(BF16), head_rows(v_ref, h))
        run_scr[...] += tot

    @pl.when(g == 0)
    def _():
        q = q_ref[0].astype(F32)
        for h in range(SB_HEADS):
            qh_scr[h] = q[:, h * HEAD_DIM:(h + 1) * HEAD_DIM]
        acc_scr[...] = jnp.zeros_like(acc_scr)
        run_scr[...] = jnp.zeros_like(run_scr)
        qi = lax.broadcasted_iota(jnp.int32, (SB_HEADS * dec, PAGE), 0) % dec
        kj = lax.broadcasted_iota(jnp.int32, (SB_HEADS * dec, PAGE), 1)
        attend(kn_ref.at[0], vn_ref.at[0], kj < qi)

    for pidx in range(ppb):
        attend(k_refs[pidx].at[0, 0], v_refs[pidx].at[0, 0], None)

    @pl.when(g == n_pages // ppb - 1)
    def _():
        o_ref[0] = jnp.concatenate([acc_scr[h] for h in range(SB_HEADS)], axis=1).astype(o_ref.dtype)


def _sb_sample(qs, k_rows, v_rows, cache_k, cache_v, layer, page_table, bias, batch):
    n = qs.shape[0]
    dec = n // batch
    n_pages = page_table.shape[1]
    ppb = 8 if n_pages % 8 == 0 else 1
    rows_n = SB_HEADS * dec
    q3 = qs.reshape(batch, dec, SB_W)

    def as_page(a):
        a = jnp.pad(a.reshape(batch, dec, SB_HEADS, HEAD_DIM), ((0, 0), (0, PAGE - dec), (0, 0), (0, 0)))
        return a.reshape(batch, PAGE * SB_HEADS, HEAD_DIM)

    bias_rows = jnp.broadcast_to(jnp.repeat(bias * LOG2E, dec)[:, None], (rows_n, PAGE))

    def page_spec(pidx):
        return pl.BlockSpec((1, 1, PAGE * SB_HEADS, HEAD_DIM),
                            lambda b, g, pt: (layer, pt[b, n_pages - 1 - (g * ppb + pidx)], 0, 0))

    const = lambda shape: pl.BlockSpec(shape, lambda b, g, pt: (0,) * len(shape))
    grid_spec = pltpu.PrefetchScalarGridSpec(
        num_scalar_prefetch=1,
        grid=(batch, n_pages // ppb),
        in_specs=[
            pl.BlockSpec((1, dec, SB_W), lambda b, g, pt: (b, 0, 0)),
            pl.BlockSpec((1, PAGE * SB_HEADS, HEAD_DIM), lambda b, g, pt: (b, 0, 0)),
            pl.BlockSpec((1, PAGE * SB_HEADS, HEAD_DIM), lambda b, g, pt: (b, 0, 0)),
            const((rows_n, PAGE)), const((PAGE, PAGE + LANES)),
        ] + [page_spec(pidx) for pidx in range(ppb)] * 2,
        out_specs=pl.BlockSpec((1, dec, SB_W), lambda b, g, pt: (b, 0, 0)),
        scratch_shapes=[
            pltpu.VMEM((SB_HEADS, dec, HEAD_DIM), F32),
            pltpu.VMEM((SB_HEADS, dec, HEAD_DIM), F32),
            pltpu.VMEM((rows_n, PAGE), F32),
        ],
    )
    o = pl.pallas_call(
        functools.partial(_sb_sample_kernel, n_pages=n_pages, ppb=ppb, dec=dec),
        grid_spec=grid_spec,
        out_shape=jax.ShapeDtypeStruct((batch, dec, SB_W), BF16),
        compiler_params=_cparams(("arbitrary", "arbitrary")),
    )(page_table, q3, as_page(k_rows), as_page(v_rows), bias_rows, _cum_matrix(PAGE),
      *([cache_k] * ppb), *([cache_v] * ppb))
    return o.reshape(n, SB_W)


def _outproj_kernel(og_ref, or_ref, os_ref, x_ref, g1_ref, sh_ref, sc_ref, ng_ref, wg_ref, wr_ref, ws_ref,
                    *rest, with_router):
    if with_router:
        router_ref, x1_ref, h2_ref, gates_ref = rest
    else:
        x1_ref, h2_ref = rest
    mix = _dot(og_ref[...], wg_ref[...]) + _dot(or_ref[...], wr_ref[...]) + _dot(os_ref[...], ws_ref[...])
    x1 = x_ref[...] + g1_ref[0] * mix
    x1_ref[...] = x1
    ms = jnp.mean(x1 * x1, axis=-1, keepdims=True)
    h2 = x1 * lax.rsqrt(ms + EPS) * ng_ref[...]
    h2 = h2 * (1.0 + sc_ref[0]) + sh_ref[0]
    h2_ref[...] = h2.astype(BF16)
    if with_router:
        n_exp = 8
        logits = _dot_hi(h2, router_ref[...])
        lane = lax.broadcasted_iota(jnp.int32, logits.shape, 1).astype(F32)
        neg = jnp.float32(-jnp.inf)
        lg = jnp.where(lane < n_exp, logits, neg)
        m1 = jnp.max(lg, axis=1, keepdims=True)
        i1 = jnp.min(jnp.where(lg == m1, lane, float(LANES)), axis=1, keepdims=True)
        lg2 = jnp.where(lane == i1, neg, lg)
        m2 = jnp.max(lg2, axis=1, keepdims=True)
        i2 = jnp.min(jnp.where(lg2 == m2, lane, float(LANES)), axis=1, keepdims=True)
        e2 = jnp.exp(m2 - m1)
        gates_ref[...] = jnp.where(lane == i1, 1.0 / (1.0 + e2), 0.0) + jnp.where(lane == i2, e2 / (1.0 + e2), 0.0)


def _outproj(o_gla, o_rwkv, o_sb, x, mod, norm_g, wg, wr, ws, router):
    n, d = x.shape
    nb = mod.shape[0]
    tm = _tile(n // nb, 512)
    tpb = (n // nb) // tm
    with_router = router is not None
    row = lambda w: pl.BlockSpec((tm, w), lambda i: (i, 0))
    const = lambda shape: pl.BlockSpec(shape, lambda i: (0,) * len(shape))
    in_specs = [row(GLA_V), row(RWKV_W), row(SB_W), row(d),
                _mod_spec(mod, tm, tpb, 2), _mod_spec(mod, tm, tpb, 3), _mod_spec(mod, tm, tpb, 4),
                const((1, d)), const((GLA_V, d)), const((RWKV_W, d)), const((SB_W, d))]
    args = [o_gla, o_rwkv, o_sb, x, mod, mod, mod, norm_g.reshape(1, d), wg, wr, ws]
    out_specs = [row(d), row(d)]
    out_shape = [jax.ShapeDtypeStruct((n, d), F32), jax.ShapeDtypeStruct((n, d), BF16)]
    if with_router:
        in_specs.append(const((d, LANES)))
        args.append(router)
        out_specs.append(row(LANES))
        out_shape.append(jax.ShapeDtypeStruct((n, LANES), F32))
    return pl.pallas_call(
        functools.partial(_outproj_kernel, with_router=with_router),
        grid=(n // tm,),
        in_specs=in_specs, out_specs=out_specs, out_shape=out_shape,
        compiler_params=_cparams(("arbitrary",)),
    )(*args)


def _ffn_kernel(h_ref, x1_ref, g2_ref, *rest, gated):
    if gated:
        gates_ref, w1_ref, w3_ref, w2_ref, o_ref, acc_scr = rest
    else:
        w1_ref, w3_ref, w2_ref, o_ref, acc_scr = rest
    e = pl.program_id(1)
    f = pl.program_id(2)

    @pl.when((e == 0) & (f == 0))
    def _():
        acc_scr[...] = jnp.zeros_like(acc_scr)

    h = h_ref[...]
    a = _dot(h, w1_ref[0])
    t = a * _sigmoid(a) * _dot(h, w3_ref[0])
    if gated:
        gates = gates_ref[...]
        lane = lax.broadcasted_iota(jnp.int32, gates.shape, 1)
        t = t * jnp.sum(jnp.where(lane == e, gates, 0.0), axis=1, keepdims=True)
    acc_scr[...] += _dot(t.astype(BF16), w2_ref[0])

    @pl.when((e == pl.num_programs(1) - 1) & (f == pl.num_programs(2) - 1))
    def _():
        o_ref[...] = x1_ref[...] + g2_ref[0] * acc_scr[...]


def _ffn(h2, x1, mod, w1, w3, w2, gates, tf):
    n, d = x1.shape
    n_exp, _, ff = w1.shape
    nb = mod.shape[0]
    tm = _tile(n // nb, 512)
    tpb = (n // nb) // tm
    gated = gates is not None
    row = lambda w: pl.BlockSpec((tm, w), lambda i, e, f: (i, 0))
    in_specs = [row(d), row(d), _mod_spec(mod, tm, tpb, 5)]
    args = [h2, x1, mod]
    if gated:
        in_specs.append(row(LANES))
        args.append(gates)
    in_specs += [
        pl.BlockSpec((1, d, tf), lambda i, e, f: (e, 0, f)),
        pl.BlockSpec((1, d, tf), lambda i, e, f: (e, 0, f)),
        pl.BlockSpec((1, tf, d), lambda i, e, f: (e, f, 0)),
    ]
    args += [w1, w3, w2]
    return pl.pallas_call(
        functools.partial(_ffn_kernel, gated=gated),
        grid=(n // tm, n_exp, ff // tf),
        in_specs=in_specs,
        out_specs=row(d),
        out_shape=jax.ShapeDtypeStruct((n, d), F32),
        scratch_shapes=[pltpu.VMEM((tm, d), F32)],
        compiler_params=_cparams(("arbitrary", "arbitrary", "arbitrary")),
    )(*args)


def _pad_rows(w, rows, offset):
    return jnp.pad(w, ((0, 0), (offset, rows - offset - w.shape[1]), (0, 0)))


def _layer(l, x, mod, batch, state, W, sb_fn, sb_bias=None):
    n = x.shape[0]
    gla0, rwkv0, shift0 = state
    p_gla, p_rwkv, p_sb = _inproj(x, mod, W['norm1_g'][l], W['w_in'][l])
    chunk = math.gcd(n // batch, GLA_CHUNK)
    o_gla, gla_st = _gla(p_gla, gla0, W['gla_wa2'][l], W['gla_ba'][l], W['gla_ng'][l], batch, chunk)
    o_rwkv, rwkv_st = _rwkv(p_rwkv, shift0, rwkv0, {k: v[l] for k, v in W['rwkv'].items()}, batch, 2)
    qs, k_rows, kb, vb = _sbprep(p_sb, W['sb_qg'][l], W['sb_kg'][l], sb_bias)
    v_rows = p_sb[:, 2 * SB_W:]
    o_sb = sb_fn(qs, kb, vb, k_rows, v_rows)
    router = W['router'][l // 2] if l % 2 else None
    outs = _outproj(o_gla, o_rwkv, o_sb, x, mod, W['norm2_g'][l], W['wo_g'][l], W['wo_r'][l], W['wo_s'][l], router)
    if l % 2 == 0:
        x1, h2 = outs
        x = _ffn(h2, x1, mod, W['ffn_w1'][l // 2][None], W['ffn_w3'][l // 2][None], W['ffn_w2'][l // 2][None],
                 None, W['ffn_w1'].shape[-1] // 2)
    else:
        x1, h2, gates = outs
        x = _ffn(h2, x1, mod, W['moe_w1'][l // 2], W['moe_w3'][l // 2], W['moe_w2'][l // 2], gates,
                 W['moe_w1'].shape[-1] // 4)
    t = n // batch
    shift_row = p_rwkv.reshape(batch, t, RWKV_COLS)[:, -1]
    return x, (k_rows, v_rows, _gla_state_from_bd(gla_st), _rwkv_state_out(rwkv_st), shift_row)


def kernel(x_prompt, x_sample, cache_sb_k, cache_sb_v, state_gla, state_rwkv, state_shift, page_table, c_prompt, c_sample, norm1_g, norm2_g, w_ada, b_ada, w_in, w_out, gla_w_a2, gla_b_a, gla_norm_g, rwkv_mu, rwkv_w0, rwkv_w_w2, rwkv_a0, rwkv_w_a2, rwkv_w_g2, rwkv_k_k, rwkv_k_a, rwkv_r_k, rwkv_ln_g, rwkv_ln_b, sb_q_g, sb_k_g, sb_bias, ffn_w1, ffn_w3, ffn_w2, moe_router, moe_w1, moe_w3, moe_w2):
    depth = w_in.shape[0]
    bp, seq, d = x_prompt.shape
    bs, dec, _ = x_sample.shape
    n_pool, page = cache_sb_k.shape[1:3]

    g0 = 0
    gq, gk, gv = w_in[:, :, g0:g0 + GLA_QK], w_in[:, :, g0 + GLA_QK:g0 + 2 * GLA_QK], w_in[:, :, g0 + 2 * GLA_QK:g0 + 2 * GLA_QK + GLA_V]
    g_low = w_in[:, :, g0 + 2 * GLA_QK + GLA_V:g0 + 2 * GLA_QK + GLA_V + GLA_LORA]
    g_r = w_in[:, :, g0 + 2 * GLA_QK + GLA_V + GLA_LORA:g0 + 2 * GLA_QK + 2 * GLA_V + GLA_LORA]
    gla_cols = 2 * GLA_QK + 2 * GLA_V + GLA_LORA
    g_low = jnp.pad(g_low, ((0, 0), (0, 0), (0, LANES - GLA_LORA)))
    w_in_r = jnp.concatenate([gq, gk, gv, g_r, g_low, w_in[:, :, gla_cols:]], axis=-1).astype(BF16)

    tile_heads = lambda g, h: jnp.tile(g, (1, h))
    W = dict(
        norm1_g=norm1_g, norm2_g=norm2_g, w_in=w_in_r,
        wo_g=w_out[:, :GLA_V].astype(BF16), wo_r=w_out[:, GLA_V:GLA_V + RWKV_W].astype(BF16),
        wo_s=w_out[:, GLA_V + RWKV_W:].astype(BF16),
        gla_wa2=_pad_rows(gla_w_a2, LANES, 0), gla_ba=gla_b_a[:, None, :],
        gla_ng=tile_heads(gla_norm_g, GLA_HEADS)[:, None, :],
        rwkv=dict(
            mu=rwkv_mu[:, None, :], w0=rwkv_w0[:, None, :], ww2=_pad_rows(rwkv_w_w2, LANES, 0),
            a0=rwkv_a0[:, None, :], wa2=_pad_rows(rwkv_w_a2, LANES, RWKV_W_LORA), wg2=rwkv_w_g2,
            kk=rwkv_k_k[:, None, :], ka=rwkv_k_a[:, None, :], rk=rwkv_r_k.reshape(depth, 1, RWKV_W),
            lng=rwkv_ln_g[:, None, :], lnb=rwkv_ln_b[:, None, :]),
        sb_qg=tile_heads(sb_q_g, SB_HEADS)[:, None, :], sb_kg=tile_heads(sb_k_g, SB_HEADS)[:, None, :],
        ffn_w1=ffn_w1.astype(BF16), ffn_w3=ffn_w3.astype(BF16), ffn_w2=ffn_w2.astype(BF16),
        router=jnp.pad(moe_router, ((0, 0), (0, 0), (0, LANES - moe_router.shape[-1]))),
        moe_w1=moe_w1.astype(BF16), moe_w3=moe_w3.astype(BF16), moe_w2=moe_w2.astype(BF16),
    )

    rows = bp + bs
    rows_pad = -(-rows // 8) * 8
    c_all = jnp.pad(jnp.concatenate([c_prompt, c_sample], axis=0), ((0, rows_pad - rows), (0, 0)))
    mod_all = _modulation(c_all, w_ada, b_ada)

    xp = x_prompt.reshape(bp * seq, d)
    xs = x_sample.reshape(bs * dec, d)
    cache_k = cache_sb_k.reshape(depth, n_pool, page * SB_HEADS, HEAD_DIM)
    cache_v = cache_sb_v.reshape(depth, n_pool, page * SB_HEADS, HEAD_DIM)
    zero_p = (jnp.zeros((bp, GLA_V, GLA_QK), F32), jnp.zeros((bp, HEAD_DIM, RWKV_W), F32),
              jnp.zeros((bp, 1, RWKV_COLS), F32))
    st_p, st_s = [], []
    for l in range(depth):
        mod_p = mod_all[l, :bp][:, None, :]
        mod_s = jnp.repeat(mod_all[l, bp:rows], dec, axis=0)[None]
        xp, sp = _layer(l, xp, mod_p, bp, zero_p, W,
                        lambda q, kb, vb, k_rows, v_rows: _sb_prompt(q, kb, vb, bp), sb_bias[l])
        state_s = (_gla_state_to_bd(state_gla[l]), _rwkv_state_in(state_rwkv[l]), state_shift[l][:, None, :])
        xs, ss = _layer(l, xs, mod_s, bs, state_s, W,
                        lambda q, kb, vb, k_rows, v_rows: _sb_sample(q, k_rows, v_rows, cache_k, cache_v, l,
                                                                     page_table, sb_bias[l], bs))
        st_p.append(sp)
        st_s.append(ss)

    def stack(sts, idx, shape):
        return jnp.stack([s[idx] for s in sts]).reshape((depth,) + shape)

    outs = [xp.reshape(bp, seq, d), xs.reshape(bs, dec, d)]
    for sts, b, t in ((st_p, bp, seq), (st_s, bs, dec)):
        outs += [
            stack(sts, 0, (b, t, SB_HEADS, HEAD_DIM)), stack(sts, 1, (b, t, SB_HEADS, HEAD_DIM)),
            stack(sts, 2, (b, GLA_HEADS, GLA_DK, GLA_DV)), stack(sts, 3, (b, RWKV_HEADS, HEAD_DIM, HEAD_DIM)),
            stack(sts, 4, (b, RWKV_COLS)),
        ]
    return tuple(outs)
```

```python
import functools
import math

import numpy as np
import jax
import jax.numpy as jnp
from jax import lax
from jax.experimental import pallas as pl
from jax.experimental.pallas import tpu as pltpu

F32 = jnp.float32
BF16 = jnp.bfloat16
HIGHEST = lax.Precision.HIGHEST

HEAD_DIM = 64
GLA_HEADS = 4
GLA_DK = 32
GLA_DV = 64
GLA_LORA = 16
GLA_TAU = 16.0
GLA_CHUNK = 16
RWKV_HEADS = 4
RWKV_W_LORA = 64
RWKV_A_LORA = 64
RWKV_G_LORA = 128
RWKV_LN_EPS = 64e-5
SB_HEADS = 8
TOP_K = 2
EPS = 1e-6
LOG2E = math.log2(math.e)

GLA_QK = GLA_HEADS * GLA_DK
GLA_V = GLA_HEADS * GLA_DV
GLA_PCOLS = 2 * GLA_QK + 2 * GLA_V + 128
RWKV_W = RWKV_HEADS * HEAD_DIM
RWKV_COLS = 3 * RWKV_W + RWKV_W_LORA + RWKV_A_LORA + RWKV_G_LORA
SB_W = SB_HEADS * HEAD_DIM
SB_COLS = 3 * SB_W
LANES = 128
PAGE = 128
VMEM_LIMIT = 56 * 1024 * 1024


def _cparams(sem):
    return pltpu.CompilerParams(dimension_semantics=sem, vmem_limit_bytes=VMEM_LIMIT)


def _tile(n, pref):
    if n <= pref:
        return n
    t = pref
    while n % t:
        t -= 8
    return t


def _block_ones(n, group, value=1.0):
    idx = np.arange(n) // group
    return jnp.asarray((idx[:, None] == idx[None, :]).astype(np.float32) * value)


def _dot(a, b):
    return jnp.dot(a, b, preferred_element_type=F32)


def _dot_hi(a, b):
    return jnp.dot(a, b, preferred_element_type=F32, precision=HIGHEST)


def _dot_nt(a, b, precision=None):
    return lax.dot_general(a, b, (((1,), (1,)), ((), ())), preferred_element_type=F32, precision=precision)


def _dot_tn(a, b, precision=None):
    return lax.dot_general(a, b, (((0,), (0,)), ((), ())), preferred_element_type=F32, precision=precision)


def _dot_split(x, m_bf16):
    hi = x.astype(BF16)
    lo = (x - hi.astype(F32)).astype(BF16)
    return _dot(hi, m_bf16) + _dot(lo, m_bf16)


def _sigmoid(x):
    return 1.0 / (1.0 + jnp.exp(-x))


def _softplus(x):
    return jnp.maximum(x, 0.0) + jnp.log(1.0 + jnp.exp(-jnp.abs(x)))


def _mod_kernel(c_ref, w_ref, b_ref, o_ref):
    c = c_ref[...]
    o_ref[0] = _dot_hi(c * _sigmoid(c), w_ref[0]) + b_ref[0]


def _modulation(c_all, w_ada, b_ada):
    depth, d, d6 = w_ada.shape
    r = c_all.shape[0]
    tn = 1024
    return pl.pallas_call(
        _mod_kernel,
        grid=(depth, d6 // tn),
        in_specs=[
            pl.BlockSpec((r, d), lambda l, j: (0, 0)),
            pl.BlockSpec((1, d, tn), lambda l, j: (l, 0, j)),
            pl.BlockSpec((1, 1, tn), lambda l, j: (l, 0, j)),
        ],
        out_specs=pl.BlockSpec((1, r, tn), lambda l, j: (l, 0, j)),
        out_shape=jax.ShapeDtypeStruct((depth, r, d6), F32),
        compiler_params=_cparams(("arbitrary", "arbitrary")),
    )(c_all, w_ada, b_ada.reshape(depth, 1, d6))


def _mod_spec(mod, tm, tpb, chunk):
    d = mod.shape[2] // 6
    if mod.shape[1] == 1:
        return pl.BlockSpec((1, 1, d), lambda i, *_: (i // tpb, 0, chunk))
    return pl.BlockSpec((1, tm, d), lambda i, *_: (i // tpb, i % tpb, chunk))


def _inproj_kernel(x_ref, sh_ref, sc_ref, g_ref, w_ref, pg_ref, pr_ref, ps_ref):
    x = x_ref[...]
    ms = jnp.mean(x * x, axis=-1, keepdims=True)
    h = x * lax.rsqrt(ms + EPS) * g_ref[...]
    h = h * (1.0 + sc_ref[0]) + sh_ref[0]
    p = _dot(h.astype(BF16), w_ref[...])
    pg_ref[...] = p[:, :GLA_PCOLS]
    pr_ref[...] = p[:, GLA_PCOLS:GLA_PCOLS + RWKV_COLS]
    ps_ref[...] = p[:, GLA_PCOLS + RWKV_COLS:]


def _inproj(x, mod, norm_g, w):
    n, d = x.shape
    nb = mod.shape[0]
    tm = _tile(n // nb, 256)
    tpb = (n // nb) // tm
    ncols = w.shape[1]
    return pl.pallas_call(
        _inproj_kernel,
        grid=(n // tm,),
        in_specs=[
            pl.BlockSpec((tm, d), lambda i: (i, 0)),
            _mod_spec(mod, tm, tpb, 0),
            _mod_spec(mod, tm, tpb, 1),
            pl.BlockSpec((1, d), lambda i: (0, 0)),
            pl.BlockSpec((d, ncols), lambda i: (0, 0)),
        ],
        out_specs=[
            pl.BlockSpec((tm, GLA_PCOLS), lambda i: (i, 0)),
            pl.BlockSpec((tm, RWKV_COLS), lambda i: (i, 0)),
            pl.BlockSpec((tm, SB_COLS), lambda i: (i, 0)),
        ],
        out_shape=[
            jax.ShapeDtypeStruct((n, GLA_PCOLS), F32),
            jax.ShapeDtypeStruct((n, RWKV_COLS), F32),
            jax.ShapeDtypeStruct((n, SB_COLS), F32),
        ],
        compiler_params=_cparams(("arbitrary",)),
    )(x, mod, mod, norm_g.reshape(1, d), w)


def _gla_kernel(p_ref, s0_ref, wa2_ref, ba_ref, ng_ref, hexp_ref, havg_ref, bd_ref, tril_ref,
                o_ref, s_ref, la_scr, *, chunk, n_chunks):
    c_len = chunk

    @pl.when(pl.program_id(1) == 0)
    def _():
        s_ref[0] = s0_ref[0]

    glow = p_ref[0, :, 2 * GLA_QK + 2 * GLA_V:]
    x = _dot_hi(glow, wa2_ref[...]) + ba_ref[...]
    la_scr[...] = -_softplus(-x) * (1.0 / GLA_TAU)
    row = lax.broadcasted_iota(jnp.int32, (c_len, 1), 0)
    hexp = hexp_ref[...]
    havg = havg_ref[...]
    bd = bd_ref[...]
    tril = tril_ref[...]
    ng = ng_ref[...]

    def body(c, carry):
        sl = pl.ds(pl.multiple_of(c * c_len, c_len), c_len)
        q = p_ref[0, sl, 0:GLA_QK] * (GLA_DK ** -0.5)
        k = p_ref[0, sl, GLA_QK:2 * GLA_QK]
        v = p_ref[0, sl, 2 * GLA_QK:2 * GLA_QK + GLA_V]
        r = p_ref[0, sl, 2 * GLA_QK + GLA_V:2 * GLA_QK + 2 * GLA_V]
        b = _dot_hi(tril, la_scr[sl, :])
        b_last = b[c_len - 1:c_len, :]
        zs = []
        for s in range(c_len):
            m = row >= s
            e = jnp.exp(jnp.where(m, b - b[s:s + 1, :], 0.0))
            zs.append(jnp.where(m, q * k[s:s + 1, :] * e, 0.0))
        a_exp = _dot_hi(jnp.concatenate(zs, axis=0), hexp)
        o = jnp.zeros((c_len, GLA_V), F32)
        for s in range(c_len):
            o = o + a_exp[s * c_len:(s + 1) * c_len, :] * v[s:s + 1, :]
        st = s_ref[0]
        o = o + _dot_nt(q * jnp.exp(b), st, HIGHEST)
        k_out = k * jnp.exp(b_last - b)
        s_ref[0] = st * jnp.exp(b_last) + _dot_tn(v, k_out, HIGHEST) * bd
        ms = _dot_hi(o * o, havg)
        on = o * lax.rsqrt(ms + EPS) * ng
        o_ref[0, sl, :] = (on * (r * _sigmoid(r))).astype(o_ref.dtype)
        return carry

    lax.fori_loop(0, n_chunks, body, 0)


def _gla(p_gla, s0t, wa2p, ba, ng_t, batch, chunk):
    n = p_gla.shape[0]
    t = n // batch
    tt = _tile(t, 512)
    n_chunks = tt // chunk
    p3 = p_gla.reshape(batch, t, GLA_PCOLS)
    hexp = jnp.asarray((np.arange(GLA_QK)[:, None] // GLA_DK == np.arange(GLA_V)[None, :] // GLA_DV).astype(np.float32))
    havg = _block_ones(GLA_V, GLA_DV, 1.0 / GLA_DV)
    bd = jnp.asarray((np.arange(GLA_V)[:, None] // GLA_DV == np.arange(GLA_QK)[None, :] // GLA_DK).astype(np.float32))
    tril = jnp.asarray(np.tril(np.ones((chunk, chunk), np.float32)))
    const = lambda shape: pl.BlockSpec(shape, lambda b, i: (0,) * len(shape))
    o, s = pl.pallas_call(
        functools.partial(_gla_kernel, chunk=chunk, n_chunks=n_chunks),
        grid=(batch, t // tt),
        in_specs=[
            pl.BlockSpec((1, tt, GLA_PCOLS), lambda b, i: (b, i, 0)),
            pl.BlockSpec((1, GLA_V, GLA_QK), lambda b, i: (b, 0, 0)),
            const((LANES, GLA_QK)), const((1, GLA_QK)), const((1, GLA_V)),
            const((GLA_QK, GLA_V)), const((GLA_V, GLA_V)), const((GLA_V, GLA_QK)), const((chunk, chunk)),
        ],
        out_specs=[
            pl.BlockSpec((1, tt, GLA_V), lambda b, i: (b, i, 0)),
            pl.BlockSpec((1, GLA_V, GLA_QK), lambda b, i: (b, 0, 0)),
        ],
        out_shape=[
            jax.ShapeDtypeStruct((batch, t, GLA_V), BF16),
            jax.ShapeDtypeStruct((batch, GLA_V, GLA_QK), F32),
        ],
        scratch_shapes=[pltpu.VMEM((tt, GLA_QK), F32)],
        compiler_params=_cparams(("arbitrary", "arbitrary")),
    )(p3, s0t, wa2p, ba, ng_t, hexp, havg, bd, tril)
    return o.reshape(n, GLA_V), s


def _gla_state_to_bd(s0):
    b = s0.shape[0]
    eye = jnp.eye(GLA_HEADS, dtype=s0.dtype)
    st = jnp.einsum('bhkv,hg->bhvgk', s0, eye)
    return st.reshape(b, GLA_V, GLA_QK)


def _gla_state_from_bd(st):
    b = st.shape[0]
    s5 = st.reshape(b, GLA_HEADS, GLA_DV, GLA_HEADS, GLA_DK)
    diag = jnp.stack([s5[:, h, :, h, :] for h in range(GLA_HEADS)], axis=1)
    return jnp.swapaxes(diag, 2, 3)


def _rwkv_kernel(p_ref, sh0_ref, s0_ref, mu_ref, w0_ref, ww2_ref, a0_ref, wa2_ref, wg2_ref, kk_ref, ka_ref,
                 rk_ref, lng_ref, lnb_ref, ones_ref, onesb_ref, avg_ref, sel_ref,
                 o_ref, s_ref, prev_scr, nkk_scr, d_scr, b_scr, k_scr, r_scr, v_scr, y_scr, *, nbb, tt):
    first = pl.program_id(1) == 0

    @pl.when(first)
    def _():
        s_ref[...] = s0_ref[...]
        prev_scr[...] = sh0_ref[...]

    ones_f = ones_ref[...]
    ones_b = onesb_ref[...]
    avg = avg_ref[...]
    sel = sel_ref[...]
    row = lax.broadcasted_iota(jnp.int32, (tt, 1), 0)
    gate = []
    for bb in range(nbb):
        p = p_ref[bb]
        p_prev = jnp.where(row == 0, prev_scr[bb], pltpu.roll(p, 1, axis=0))
        prev_scr[bb] = p[tt - 1:tt, :]
        xm = p + (p_prev - p) * mu_ref[...]
        r = xm[:, 0:RWKV_W]
        k = xm[:, RWKV_W:2 * RWKV_W]
        v = xm[:, 2 * RWKV_W:3 * RWKV_W]
        xwa = xm[:, 3 * RWKV_W:3 * RWKV_W + LANES]
        xg = xm[:, 3 * RWKV_W + LANES:]
        w = -_softplus(-(w0_ref[...] + _dot_hi(jnp.tanh(xwa), ww2_ref[...]))) - 0.5
        a = _sigmoid(a0_ref[...] + _dot_hi(xwa, wa2_ref[...]))
        gate.append(_dot_hi(_sigmoid(xg), wg2_ref[...]))
        kk = k * kk_ref[...]
        nrm = jnp.sqrt(_dot_hi(kk * kk, ones_f))
        kk = kk / jnp.maximum(nrm, 1e-12)
        k2 = k * (1.0 + (a - 1.0) * ka_ref[...])
        nkk_scr[bb] = -kk
        d_scr[bb] = jnp.exp(-jnp.exp(w))
        b_scr[bb] = kk * a
        k_scr[bb] = k2
        r_scr[bb] = r
        v_scr[bb] = v

    def read_out(s, bb, ts):
        yb = _dot_split(s * jnp.broadcast_to(r_scr[bb, ts, :], (HEAD_DIM, RWKV_W)), ones_b)
        y_scr[bb, ts, :] = jnp.sum(yb * sel, axis=0, keepdims=True)

    def step(t, states):
        new = []
        for bb in range(nbb):
            s = states[bb]
            ts = pl.ds(t, 1)
            tp = pl.ds(jnp.maximum(t - 1, 0), 1)
            bc = lambda ref, rows: jnp.broadcast_to(ref[bb, rows, :], (HEAD_DIM, RWKV_W))
            lhs = jnp.concatenate([s * bc(nkk_scr, ts), s * bc(r_scr, tp), sel * bc(v_scr, ts)], axis=0)
            res = _dot_split(lhs, ones_b)
            sa = res[0:HEAD_DIM]
            y_scr[bb, tp, :] = jnp.sum(res[HEAD_DIM:2 * HEAD_DIM] * sel, axis=0, keepdims=True)
            vb = res[2 * HEAD_DIM:]
            new.append(s * bc(d_scr, ts) + sa * bc(b_scr, ts) + vb * bc(k_scr, ts))
        return tuple(new)

    states = lax.fori_loop(0, tt, step, tuple(s_ref[bb] for bb in range(nbb)))
    for bb in range(nbb):
        s_ref[bb] = states[bb]
        read_out(states[bb], bb, pl.ds(tt - 1, 1))
        y = y_scr[bb]
        mean = _dot_hi(y, avg)
        yc = y - mean
        var = _dot_hi(yc * yc, avg)
        yn = yc * lax.rsqrt(var + RWKV_LN_EPS) * lng_ref[...] + lnb_ref[...]
        bonus = _dot_hi(r_scr[bb] * k_scr[bb] * rk_ref[...], ones_f) * v_scr[bb]
        o_ref[bb] = ((yn + bonus) * gate[bb]).astype(o_ref.dtype)


def _rwkv(p_rwkv, shift0, s0, wl, batch, nbb):
    n = p_rwkv.shape[0]
    t = n // batch
    tt = _tile(t, 256)
    p3 = p_rwkv.reshape(batch, t, RWKV_COLS)
    ones_f = _block_ones(RWKV_W, HEAD_DIM)
    avg = _block_ones(RWKV_W, HEAD_DIM, 1.0 / HEAD_DIM)
    sel = jnp.asarray((np.arange(HEAD_DIM)[:, None] == np.arange(RWKV_W)[None, :] % HEAD_DIM).astype(np.float32))
    const = lambda shape: pl.BlockSpec(shape, lambda b, i: (0,) * len(shape))
    row = const((1, RWKV_W))
    scr = lambda: pltpu.VMEM((nbb, tt, RWKV_W), F32)
    o, s = pl.pallas_call(
        functools.partial(_rwkv_kernel, nbb=nbb, tt=tt),
        grid=(batch // nbb, t // tt),
        in_specs=[
            pl.BlockSpec((nbb, tt, RWKV_COLS), lambda b, i: (b, i, 0)),
            pl.BlockSpec((nbb, 1, RWKV_COLS), lambda b, i: (b, 0, 0)),
            pl.BlockSpec((nbb, HEAD_DIM, RWKV_W), lambda b, i: (b, 0, 0)),
            const((1, RWKV_COLS)), row, const((LANES, RWKV_W)), row, const((LANES, RWKV_W)),
            const((RWKV_G_LORA, RWKV_W)), row, row, row, row, row,
            const((RWKV_W, RWKV_W)), const((RWKV_W, RWKV_W)), const((RWKV_W, RWKV_W)), const((HEAD_DIM, RWKV_W)),
        ],
        out_specs=[
            pl.BlockSpec((nbb, tt, RWKV_W), lambda b, i: (b, i, 0)),
            pl.BlockSpec((nbb, HEAD_DIM, RWKV_W), lambda b, i: (b, 0, 0)),
        ],
        out_shape=[
            jax.ShapeDtypeStruct((batch, t, RWKV_W), BF16),
            jax.ShapeDtypeStruct((batch, HEAD_DIM, RWKV_W), F32),
        ],
        scratch_shapes=[pltpu.VMEM((nbb, 1, RWKV_COLS), F32)] + [scr() for _ in range(7)],
        compiler_params=_cparams(("arbitrary", "arbitrary")),
    )(p3, shift0, s0, wl['mu'], wl['w0'], wl['ww2'], wl['a0'], wl['wa2'], wl['wg2'], wl['kk'], wl['ka'],
      wl['rk'], wl['lng'], wl['lnb'], ones_f, ones_f.astype(BF16), avg, sel)
    return o.reshape(n, RWKV_W), s


def _rwkv_state_in(s0):
    b = s0.shape[0]
    return jnp.transpose(s0, (0, 2, 1, 3)).reshape(b, HEAD_DIM, RWKV_W)


def _rwkv_state_out(s):
    b = s.shape[0]
    return jnp.transpose(s.reshape(b, HEAD_DIM, RWKV_HEADS, HEAD_DIM), (0, 2, 1, 3))


def _sbprep_kernel(p_ref, qg_ref, kg_ref, avg_ref, *rest, aug):
    if aug:
        qrow_ref, krow_ref, q_ref, k_ref, kb_ref, vb_ref = rest
    else:
        q_ref, k_ref, kb_ref, vb_ref = rest
    avg = avg_ref[...]
    q = p_ref[:, 0:SB_W]
    k = p_ref[:, SB_W:2 * SB_W]
    qn = q * lax.rsqrt(_dot_hi(q * q, avg) + EPS) * (qg_ref[...] * (HEAD_DIM ** -0.5 * LOG2E))
    kn = k * lax.rsqrt(_dot_hi(k * k, avg) + EPS) * kg_ref[...]
    k_ref[...] = kn
    vb_ref[...] = p_ref[:, 2 * SB_W:].astype(BF16)
    if aug:
        lo = lax.broadcasted_iota(jnp.int32, (1, LANES), 1) < HEAD_DIM

        def widen(x, row):
            parts = []
            for h in range(SB_HEADS):
                blk = x[:, (h // 2) * LANES:(h // 2 + 1) * LANES]
                if h % 2:
                    blk = pltpu.roll(blk, HEAD_DIM, axis=1)
                parts.append(jnp.where(lo, blk, 0.0))
            return (jnp.concatenate(parts, axis=1) + row).astype(BF16)

        q_ref[...] = widen(qn, qrow_ref[...])
        kb_ref[...] = widen(kn, krow_ref[...])
    else:
        q_ref[...] = qn.astype(BF16)
        kb_ref[...] = kn.astype(BF16)


def _sbprep(p_sb, qg_t, kg_t, bias=None):
    n = p_sb.shape[0]
    tm = _tile(n, 512)
    aug = bias is not None
    avg = _block_ones(SB_W, HEAD_DIM, 1.0 / HEAD_DIM)
    wide = SB_HEADS * LANES if aug else SB_W
    out = lambda w: pl.BlockSpec((tm, w), lambda i: (i, 0))
    const = lambda shape: pl.BlockSpec(shape, lambda i: (0,) * len(shape))
    in_specs = [pl.BlockSpec((tm, SB_COLS), lambda i: (i, 0)), const((1, SB_W)), const((1, SB_W)), const((SB_W, SB_W))]
    args = [p_sb, qg_t, kg_t, avg]
    if aug:
        b2 = bias * LOG2E
        hi = b2.astype(BF16).astype(F32)
        lo = (b2 - hi).astype(BF16).astype(F32)
        qrow = jnp.zeros((SB_HEADS, LANES), F32).at[:, HEAD_DIM].set(hi).at[:, HEAD_DIM + 1].set(lo)
        krow = jnp.zeros((SB_HEADS, LANES), F32).at[:, HEAD_DIM:HEAD_DIM + 2].set(1.0)
        in_specs += [const((1, wide)), const((1, wide))]
        args += [qrow.reshape(1, wide), krow.reshape(1, wide)]
    return pl.pallas_call(
        functools.partial(_sbprep_kernel, aug=aug),
        grid=(n // tm,),
        in_specs=in_specs,
        out_specs=[out(wide), out(SB_W), out(wide), out(SB_W)],
        out_shape=[
            jax.ShapeDtypeStruct((n, wide), BF16),
            jax.ShapeDtypeStruct((n, SB_W), F32),
            jax.ShapeDtypeStruct((n, wide), BF16),
            jax.ShapeDtypeStruct((n, SB_W), BF16),
        ],
        compiler_params=_cparams(("arbitrary",)),
    )(*args)


def _sb_keep(z, mask):
    keep = jnp.maximum(z, 0.0) + jnp.log2(1.0 + jnp.exp2(-jnp.abs(z)))
    if mask is not None:
        keep = jnp.where(mask, keep, 0.0)
    return z - keep, keep.astype(BF16)


def _sb_logits_stage(z, mask, cum):
    zk, kb = _sb_keep(z, mask)
    tk = z.shape[1]
    cs = _dot(kb, cum)
    return zk - cs[:, :tk], cs[:, tk:]


def _sb_weights(part, r_run, mask):
    reps = part.shape[1] // r_run.shape[1]
    e = jnp.exp2(part - (jnp.concatenate([r_run] * reps, axis=1) if reps > 1 else r_run))
    if mask is not None:
        e = jnp.where(mask, e, 0.0)
    return e


def _sb_prompt_kernel(q_ref, k_ref, v_ref, cum_ref, o_ref, zk_scr, kb_scr, *, tq):
    i = pl.program_id(2)
    cum = cum_ref[...]
    rows = lax.broadcasted_iota(jnp.int32, (tq, tq), 0)
    cols = lax.broadcasted_iota(jnp.int32, (tq, tq), 1)
    diag = cols < rows

    def stage1(j, mask):
        ks = pl.ds(pl.multiple_of(j * tq, tq), tq)
        return [_sb_keep(_dot_nt(q_ref[:, u * LANES:(u + 1) * LANES], k_ref[ks, u * LANES:(u + 1) * LANES]), mask)
                for u in range(2)]

    def stage2(j, pend, carry, mask, valid):
        vb = v_ref[pl.ds(pl.multiple_of(j * tq, tq), tq), :]
        out = []
        for u in range(2):
            zk, kb = pend[u]
            acc, r_run = carry[u]
            cs = _dot(kb, cum)
            tot = cs[:, tq:]
            pv = _dot(_sb_weights(zk - cs[:, :tq], r_run, mask).astype(BF16), vb)
            if valid is not None:
                pv = jnp.where(valid, pv, 0.0)
                tot = jnp.where(valid, tot, 0.0)
            out.append((acc + pv, r_run + tot))
        return tuple(out)

    carry = tuple((jnp.zeros((tq, LANES), F32), jnp.zeros((tq, LANES), F32)) for _ in range(2))
    carry = stage2(i, stage1(i, diag), carry, diag, None)
    zk_scr[...] = jnp.zeros_like(zk_scr)
    kb_scr[...] = jnp.zeros_like(kb_scr)

    def body(n, carry):
        pend = [(zk_scr[u], kb_scr[u]) for u in range(2)]
        carry = stage2(jnp.maximum(i - n, 0), pend, carry, None, n >= 1)
        nxt = stage1(jnp.maximum(i - 1 - n, 0), None)
        for u in range(2):
            zk_scr[u] = nxt[u][0]
            kb_scr[u] = nxt[u][1]
        return carry

    carry = lax.fori_loop(0, i + 1, body, carry)
    lo = lax.broadcasted_iota(jnp.int32, (1, LANES), 1) < HEAD_DIM
    o_ref[...] = jnp.where(lo, carry[0][0], carry[1][0]).astype(o_ref.dtype)


def _cum_matrix(tk):
    j = np.arange(tk)
    later = (j[:, None] > j[None, :]).astype(np.float32)
    return jnp.asarray(np.concatenate([later, np.ones((tk, LANES), np.float32)], axis=1)).astype(BF16)


def _sb_prompt(q_aug, k_aug, vb, batch):
    n = vb.shape[0]
    t = n // batch
    tq = _tile(t, 256)
    tpb = t // tq
    return pl.pallas_call(
        functools.partial(_sb_prompt_kernel, tq=tq),
        grid=(batch, SB_HEADS // 2, tpb),
        in_specs=[
            pl.BlockSpec((tq, 2 * LANES), lambda b, h, i: (b * tpb + i, h)),
            pl.BlockSpec((t, 2 * LANES), lambda b, h, i: (b, h)),
            pl.BlockSpec((t, LANES), lambda b, h, i: (b, h)),
            pl.BlockSpec((tq, tq + LANES), lambda b, h, i: (0, 0)),
        ],
        out_specs=pl.BlockSpec((tq, LANES), lambda b, h, i: (b * tpb + i, h)),
        out_shape=jax.ShapeDtypeStruct((n, SB_W), BF16),
        scratch_shapes=[pltpu.VMEM((2, tq, tq), F32), pltpu.VMEM((2, tq, tq), BF16)],
        compiler_params=_cparams(("arbitrary", "arbitrary", "arbitrary")),
    )(q_aug, k_aug, vb, _cum_matrix(tq))


def _sb_sample_kernel(pt_ref, q_ref, kn_ref, vn_ref, bias_ref, bd_ref, cum_ref, *rest, n_pages, ppb, dec):
    k_refs = rest[:ppb]
    v_refs = rest[ppb:2 * ppb]
    o_ref, qbd_scr, acc_scr, run_scr = rest[2 * ppb:]
    g = pl.program_id(1)
    rows_n = SB_HEADS * dec
    bd = bd_ref[...]
    cum = cum_ref[...]
    bias = bias_ref[...]

    def logits(kt, mask):
        return _sb_logits_stage(_dot(qbd_scr[...], kt.astype(BF16)) + bias, mask, cum)

    def attend(kt, vt, mask):
        part, tot = logits(kt, mask)
        acc_scr[...] += _dot_nt(_sb_weights(part, run_scr[...], mask).astype(BF16), vt.astype(BF16))
        run_scr[...] += tot

    @pl.when(g == 0)
    def _():
        q = q_ref[0].astype(F32)
        qbd_scr[...] = (jnp.concatenate([q] * SB_HEADS, axis=0) * bd).astype(BF16)
        acc_scr[...] = jnp.zeros_like(acc_scr)
        run_scr[...] = jnp.zeros_like(run_scr)
        qi = lax.broadcasted_iota(jnp.int32, (rows_n, PAGE), 0) % dec
        kj = lax.broadcasted_iota(jnp.int32, (rows_n, PAGE), 1)
        attend(kn_ref[0], vn_ref[0], kj < qi)

    pend = [logits(k_refs[pidx][0, 0].reshape(SB_W, PAGE), None) for pidx in range(ppb)]
    r_run = run_scr[...]
    acc = acc_scr[...]
    for pidx in range(ppb):
        part, tot = pend[pidx]
        vt = v_refs[pidx][0, 0].reshape(SB_W, PAGE).astype(BF16)
        acc = acc + _dot_nt(_sb_weights(part, r_run, None).astype(BF16), vt)
        r_run = r_run + tot
    run_scr[...] = r_run
    acc_scr[...] = acc

    @pl.when(g == n_pages // ppb - 1)
    def _():
        masked = acc_scr[...] * bd
        o = masked[0:dec, :]
        for h in range(1, SB_HEADS):
            o = o + masked[h * dec:(h + 1) * dec, :]
        o_ref[0] = o.astype(o_ref.dtype)


def _sb_sample(qs, k_rows, v_rows, cache_k, cache_v, layer, page_table, bias, batch):
    n = qs.shape[0]
    dec = n // batch
    n_pages = page_table.shape[1]
    ppb = 8 if n_pages % 8 == 0 else 1
    rows_n = SB_HEADS * dec
    q3 = qs.reshape(batch, dec, SB_W)

    def as_page(a):
        a = jnp.swapaxes(a.reshape(batch, dec, SB_W), 1, 2)
        return jnp.pad(a, ((0, 0), (0, 0), (0, PAGE - dec)))

    bias_rows = jnp.broadcast_to(jnp.repeat(bias * LOG2E, dec)[:, None], (rows_n, PAGE))
    bd = jnp.asarray((np.arange(rows_n)[:, None] // dec == np.arange(SB_W)[None, :] // HEAD_DIM).astype(np.float32))

    def page_spec(pidx):
        return pl.BlockSpec((1, 1, SB_HEADS, HEAD_DIM, PAGE),
                            lambda b, g, pt: (layer, pt[b, n_pages - 1 - (g * ppb + pidx)], 0, 0, 0))

    const = lambda shape: pl.BlockSpec(shape, lambda b, g, pt: (0,) * len(shape))
    grid_spec = pltpu.PrefetchScalarGridSpec(
        num_scalar_prefetch=1,
        grid=(batch, n_pages // ppb),
        in_specs=[
            pl.BlockSpec((1, dec, SB_W), lambda b, g, pt: (b, 0, 0)),
            pl.BlockSpec((1, SB_W, PAGE), lambda b, g, pt: (b, 0, 0)),
            pl.BlockSpec((1, SB_W, PAGE), lambda b, g, pt: (b, 0, 0)),
            const((rows_n, PAGE)), const((rows_n, SB_W)), const((PAGE, PAGE + LANES)),
        ] + [page_spec(pidx) for pidx in range(ppb)] * 2,
        out_specs=pl.BlockSpec((1, dec, SB_W), lambda b, g, pt: (b, 0, 0)),
        scratch_shapes=[
            pltpu.VMEM((rows_n, SB_W), BF16),
            pltpu.VMEM((rows_n, SB_W), F32),
            pltpu.VMEM((rows_n, PAGE), F32),
        ],
    )
    o = pl.pallas_call(
        functools.partial(_sb_sample_kernel, n_pages=n_pages, ppb=ppb, dec=dec),
        grid_spec=grid_spec,
        out_shape=jax.ShapeDtypeStruct((batch, dec, SB_W), BF16),
        compiler_params=_cparams(("arbitrary", "arbitrary")),
    )(page_table, q3, as_page(k_rows), as_page(v_rows), bias_rows, bd, _cum_matrix(PAGE),
      *([cache_k] * ppb), *([cache_v] * ppb))
    return o.reshape(n, SB_W)


def _outproj_kernel(og_ref, or_ref, os_ref, x_ref, g1_ref, sh_ref, sc_ref, ng_ref, wg_ref, wr_ref, ws_ref,
                    *rest, with_router):
    if with_router:
        router_ref, x1_ref, h2_ref, gates_ref = rest
    else:
        x1_ref, h2_ref = rest
    mix = _dot(og_ref[...], wg_ref[...]) + _dot(or_ref[...], wr_ref[...]) + _dot(os_ref[...], ws_ref[...])
    x1 = x_ref[...] + g1_ref[0] * mix
    x1_ref[...] = x1
    ms = jnp.mean(x1 * x1, axis=-1, keepdims=True)
    h2 = x1 * lax.rsqrt(ms + EPS) * ng_ref[...]
    h2 = h2 * (1.0 + sc_ref[0]) + sh_ref[0]
    h2_ref[...] = h2.astype(BF16)
    if with_router:
        n_exp = 8
        logits = _dot_hi(h2, router_ref[...])
        lane = lax.broadcasted_iota(jnp.int32, logits.shape, 1).astype(F32)
        neg = jnp.float32(-jnp.inf)
        lg = jnp.where(lane < n_exp, logits, neg)
        m1 = jnp.max(lg, axis=1, keepdims=True)
        i1 = jnp.min(jnp.where(lg == m1, lane, float(LANES)), axis=1, keepdims=True)
        lg2 = jnp.where(lane == i1, neg, lg)
        m2 = jnp.max(lg2, axis=1, keepdims=True)
        i2 = jnp.min(jnp.where(lg2 == m2, lane, float(LANES)), axis=1, keepdims=True)
        e2 = jnp.exp(m2 - m1)
        gates_ref[...] = jnp.where(lane == i1, 1.0 / (1.0 + e2), 0.0) + jnp.where(lane == i2, e2 / (1.0 + e2), 0.0)


def _outproj(o_gla, o_rwkv, o_sb, x, mod, norm_g, wg, wr, ws, router):
    n, d = x.shape
    nb = mod.shape[0]
    tm = _tile(n // nb, 512)
    tpb = (n // nb) // tm
    with_router = router is not None
    row = lambda w: pl.BlockSpec((tm, w), lambda i: (i, 0))
    const = lambda shape: pl.BlockSpec(shape, lambda i: (0,) * len(shape))
    in_specs = [row(GLA_V), row(RWKV_W), row(SB_W), row(d),
                _mod_spec(mod, tm, tpb, 2), _mod_spec(mod, tm, tpb, 3), _mod_spec(mod, tm, tpb, 4),
                const((1, d)), const((GLA_V, d)), const((RWKV_W, d)), const((SB_W, d))]
    args = [o_gla, o_rwkv, o_sb, x, mod, mod, mod, norm_g.reshape(1, d), wg, wr, ws]
    out_specs = [row(d), row(d)]
    out_shape = [jax.ShapeDtypeStruct((n, d), F32), jax.ShapeDtypeStruct((n, d), BF16)]
    if with_router:
        in_specs.append(const((d, LANES)))
        args.append(router)
        out_specs.append(row(LANES))
        out_shape.append(jax.ShapeDtypeStruct((n, LANES), F32))
    return pl.pallas_call(
        functools.partial(_outproj_kernel, with_router=with_router),
        grid=(n // tm,),
        in_specs=in_specs, out_specs=out_specs, out_shape=out_shape,
        compiler_params=_cparams(("arbitrary",)),
    )(*args)


def _ffn_kernel(h_ref, x1_ref, g2_ref, *rest, gated):
    if gated:
        gates_ref, w1_ref, w3_ref, w2_ref, o_ref, acc_scr = rest
    else:
        w1_ref, w3_ref, w2_ref, o_ref, acc_scr = rest
    e = pl.program_id(1)
    f = pl.program_id(2)

    @pl.when((e == 0) & (f == 0))
    def _():
        acc_scr[...] = jnp.zeros_like(acc_scr)

    h = h_ref[...]
    a = _dot(h, w1_ref[0])
    t = a * _sigmoid(a) * _dot(h, w3_ref[0])
    if gated:
        gates = gates_ref[...]
        lane = lax.broadcasted_iota(jnp.int32, gates.shape, 1)
        t = t * jnp.sum(jnp.where(lane == e, gates, 0.0), axis=1, keepdims=True)
    acc_scr[...] += _dot(t.astype(BF16), w2_ref[0])

    @pl.when((e == pl.num_programs(1) - 1) & (f == pl.num_programs(2) - 1))
    def _():
        o_ref[...] = x1_ref[...] + g2_ref[0] * acc_scr[...]


def _ffn(h2, x1, mod, w1, w3, w2, gates, tf):
    n, d = x1.shape
    n_exp, _, ff = w1.shape
    nb = mod.shape[0]
    tm = _tile(n // nb, 512)
    tpb = (n // nb) // tm
    gated = gates is not None
    row = lambda w: pl.BlockSpec((tm, w), lambda i, e, f: (i, 0))
    in_specs = [row(d), row(d), _mod_spec(mod, tm, tpb, 5)]
    args = [h2, x1, mod]
    if gated:
        in_specs.append(row(LANES))
        args.append(gates)
    in_specs += [
        pl.BlockSpec((1, d, tf), lambda i, e, f: (e, 0, f)),
        pl.BlockSpec((1, d, tf), lambda i, e, f: (e, 0, f)),
        pl.BlockSpec((1, tf, d), lambda i, e, f: (e, f, 0)),
    ]
    args += [w1, w3, w2]
    return pl.pallas_call(
        functools.partial(_ffn_kernel, gated=gated),
        grid=(n // tm, n_exp, ff // tf),
        in_specs=in_specs,
        out_specs=row(d),
        out_shape=jax.ShapeDtypeStruct((n, d), F32),
        scratch_shapes=[pltpu.VMEM((tm, d), F32)],
        compiler_params=_cparams(("arbitrary", "arbitrary", "arbitrary")),
    )(*args)


def _moe_kernel(cnt_ref, h_ref, x1_ref, g2_ref, gates_ref, pos_ref, w1_ref, w3_ref, w2_ref, o_ref,
                hc_scr, accc_scr, *, tm, ch):
    i = pl.program_id(0)
    e = pl.program_id(1)
    f = pl.program_id(2)
    last_f = f == pl.num_programs(2) - 1
    count = cnt_ref[i, e]
    rank = lax.broadcasted_iota(jnp.int32, (ch, 1), 0)

    def one_hot(c):
        return jnp.where(pos_ref[0, pl.ds(e, 1), :] == rank + c * ch, 1.0, 0.0).astype(BF16)

    @pl.when((e == 0) & (f == 0))
    def _():
        o_ref[...] = jnp.zeros_like(o_ref)

    for c in range(-(-tm // ch)):
        rows = pl.ds(c * ch, ch)

        @pl.when(c * ch < count)
        def _():
            @pl.when(f == 0)
            def _():
                hc_scr[rows, :] = _dot(one_hot(c), h_ref[...]).astype(BF16)
                accc_scr[rows, :] = jnp.zeros((ch, accc_scr.shape[1]), F32)

            hc = hc_scr[rows, :]
            a = _dot(hc, w1_ref[0])
            t = a * _sigmoid(a) * _dot(hc, w3_ref[0])
            accc_scr[rows, :] += _dot(t.astype(BF16), w2_ref[0])

            @pl.when(last_f)
            def _():
                y = accc_scr[rows, :]
                hi = y.astype(BF16)
                lo = (y - hi.astype(F32)).astype(BF16)
                p = one_hot(c)
                gates = gates_ref[...]
                lane = lax.broadcasted_iota(jnp.int32, gates.shape, 1)
                ge = jnp.sum(jnp.where(lane == e, gates, 0.0), axis=1, keepdims=True)
                o_ref[...] += ge * (_dot_tn(p, hi) + _dot_tn(p, lo))

    @pl.when((e == pl.num_programs(1) - 1) & last_f)
    def _():
        o_ref[...] = x1_ref[...] + g2_ref[0] * o_ref[...]


def _moe(h2, x1, mod, w1, w3, w2, gates, tf):
    n, d = x1.shape
    n_exp, _, ff = w1.shape
    nb = mod.shape[0]
    tm = _tile(n // nb, 1024)
    tpb = (n // nb) // tm
    ch = -(-int(tm * TOP_K / n_exp * 1.125) // 32) * 32 if tm >= 512 else tm // 2
    slots = -(-tm // ch)
    n_tiles = n // tm
    routed = (gates[:, :n_exp] > 0.0).reshape(n_tiles, tm, n_exp)
    counts = jnp.sum(routed, axis=1).astype(jnp.int32)
    ranks = jnp.cumsum(routed.astype(jnp.int32), axis=1) - 1
    pos = jnp.swapaxes(jnp.where(routed, ranks, -1), 1, 2)
    row = lambda w: pl.BlockSpec((tm, w), lambda i, e, f, cnt: (i, 0))
    grid_spec = pltpu.PrefetchScalarGridSpec(
        num_scalar_prefetch=1,
        grid=(n_tiles, n_exp, ff // tf),
        in_specs=[
            row(d), row(d), _mod_spec(mod, tm, tpb, 5), row(LANES),
            pl.BlockSpec((1, n_exp, tm), lambda i, e, f, cnt: (i, 0, 0)),
            pl.BlockSpec((1, d, tf), lambda i, e, f, cnt: (e, 0, f)),
            pl.BlockSpec((1, d, tf), lambda i, e, f, cnt: (e, 0, f)),
            pl.BlockSpec((1, tf, d), lambda i, e, f, cnt: (e, f, 0)),
        ],
        out_specs=row(d),
        scratch_shapes=[pltpu.VMEM((slots * ch, d), BF16), pltpu.VMEM((slots * ch, d), F32)],
    )
    return pl.pallas_call(
        functools.partial(_moe_kernel, tm=tm, ch=ch),
        grid_spec=grid_spec,
        out_shape=jax.ShapeDtypeStruct((n, d), F32),
        compiler_params=_cparams(("arbitrary", "arbitrary", "arbitrary")),
    )(counts, h2, x1, mod, gates, pos, w1, w3, w2)


def _pad_rows(w, rows, offset):
    return jnp.pad(w, ((0, 0), (offset, rows - offset - w.shape[1]), (0, 0)))


def _layer(l, x, mod, batch, state, W, sb_fn, sb_bias=None):
    n = x.shape[0]
    gla0, rwkv0, shift0 = state
    p_gla, p_rwkv, p_sb = _inproj(x, mod, W['norm1_g'][l], W['w_in'][l])
    chunk = math.gcd(n // batch, GLA_CHUNK)
    o_gla, gla_st = _gla(p_gla, gla0, W['gla_wa2'][l], W['gla_ba'][l], W['gla_ng'][l], batch, chunk)
    o_rwkv, rwkv_st = _rwkv(p_rwkv, shift0, rwkv0, {k: v[l] for k, v in W['rwkv'].items()}, batch, 2)
    qs, k_rows, kb, vb = _sbprep(p_sb, W['sb_qg'][l], W['sb_kg'][l], sb_bias)
    v_rows = p_sb[:, 2 * SB_W:]
    o_sb = sb_fn(qs, kb, vb, k_rows, v_rows)
    router = W['router'][l // 2] if l % 2 else None
    outs = _outproj(o_gla, o_rwkv, o_sb, x, mod, W['norm2_g'][l], W['wo_g'][l], W['wo_r'][l], W['wo_s'][l], router)
    if l % 2 == 0:
        x1, h2 = outs
        x = _ffn(h2, x1, mod, W['ffn_w1'][l // 2][None], W['ffn_w3'][l // 2][None], W['ffn_w2'][l // 2][None],
                 None, W['ffn_w1'].shape[-1] // 2)
    else:
        x1, h2, gates = outs
        x = _moe(h2, x1, mod, W['moe_w1'][l // 2], W['moe_w3'][l // 2], W['moe_w2'][l // 2], gates,
                 W['moe_w1'].shape[-1] // 7)
    t = n // batch
    shift_row = p_rwkv.reshape(batch, t, RWKV_COLS)[:, -1]
    return x, (k_rows, v_rows, _gla_state_from_bd(gla_st), _rwkv_state_out(rwkv_st), shift_row)


def kernel(x_prompt, x_sample, cache_sb_k, cache_sb_v, state_gla, state_rwkv, state_shift, page_table, c_prompt, c_sample, norm1_g, norm2_g, w_ada, b_ada, w_in, w_out, gla_w_a2, gla_b_a, gla_norm_g, rwkv_mu, rwkv_w0, rwkv_w_w2, rwkv_a0, rwkv_w_a2, rwkv_w_g2, rwkv_k_k, rwkv_k_a, rwkv_r_k, rwkv_ln_g, rwkv_ln_b, sb_q_g, sb_k_g, sb_bias, ffn_w1, ffn_w3, ffn_w2, moe_router, moe_w1, moe_w3, moe_w2):
    depth = w_in.shape[0]
    bp, seq, d = x_prompt.shape
    bs, dec, _ = x_sample.shape
    n_pool, page = cache_sb_k.shape[1:3]

    g0 = 0
    gq, gk, gv = w_in[:, :, g0:g0 + GLA_QK], w_in[:, :, g0 + GLA_QK:g0 + 2 * GLA_QK], w_in[:, :, g0 + 2 * GLA_QK:g0 + 2 * GLA_QK + GLA_V]
    g_low = w_in[:, :, g0 + 2 * GLA_QK + GLA_V:g0 + 2 * GLA_QK + GLA_V + GLA_LORA]
    g_r = w_in[:, :, g0 + 2 * GLA_QK + GLA_V + GLA_LORA:g0 + 2 * GLA_QK + 2 * GLA_V + GLA_LORA]
    gla_cols = 2 * GLA_QK + 2 * GLA_V + GLA_LORA
    g_low = jnp.pad(g_low, ((0, 0), (0, 0), (0, LANES - GLA_LORA)))
    w_in_r = jnp.concatenate([gq, gk, gv, g_r, g_low, w_in[:, :, gla_cols:]], axis=-1).astype(BF16)

    tile_heads = lambda g, h: jnp.tile(g, (1, h))
    W = dict(
        norm1_g=norm1_g, norm2_g=norm2_g, w_in=w_in_r,
        wo_g=w_out[:, :GLA_V].astype(BF16), wo_r=w_out[:, GLA_V:GLA_V + RWKV_W].astype(BF16),
        wo_s=w_out[:, GLA_V + RWKV_W:].astype(BF16),
        gla_wa2=_pad_rows(gla_w_a2, LANES, 0), gla_ba=gla_b_a[:, None, :],
        gla_ng=tile_heads(gla_norm_g, GLA_HEADS)[:, None, :],
        rwkv=dict(
            mu=rwkv_mu[:, None, :], w0=rwkv_w0[:, None, :], ww2=_pad_rows(rwkv_w_w2, LANES, 0),
            a0=rwkv_a0[:, None, :], wa2=_pad_rows(rwkv_w_a2, LANES, RWKV_W_LORA), wg2=rwkv_w_g2,
            kk=rwkv_k_k[:, None, :], ka=rwkv_k_a[:, None, :], rk=rwkv_r_k.reshape(depth, 1, RWKV_W),
            lng=rwkv_ln_g[:, None, :], lnb=rwkv_ln_b[:, None, :]),
        sb_qg=tile_heads(sb_q_g, SB_HEADS)[:, None, :], sb_kg=tile_heads(sb_k_g, SB_HEADS)[:, None, :],
        ffn_w1=ffn_w1.astype(BF16), ffn_w3=ffn_w3.astype(BF16), ffn_w2=ffn_w2.astype(BF16),
        router=jnp.pad(moe_router, ((0, 0), (0, 0), (0, LANES - moe_router.shape[-1]))),
        moe_w1=moe_w1.astype(BF16), moe_w3=moe_w3.astype(BF16), moe_w2=moe_w2.astype(BF16),
    )

    rows = bp + bs
    rows_pad = -(-rows // 8) * 8
    c_all = jnp.pad(jnp.concatenate([c_prompt, c_sample], axis=0), ((0, rows_pad - rows), (0, 0)))
    mod_all = _modulation(c_all, w_ada, b_ada)

    xp = x_prompt.reshape(bp * seq, d)
    xs = x_sample.reshape(bs * dec, d)
    cache_k = jnp.transpose(cache_sb_k, (0, 1, 3, 4, 2))
    cache_v = jnp.transpose(cache_sb_v, (0, 1, 3, 4, 2))
    zero_p = (jnp.zeros((bp, GLA_V, GLA_QK), F32), jnp.zeros((bp, HEAD_DIM, RWKV_W), F32),
              jnp.zeros((bp, 1, RWKV_COLS), F32))
    st_p, st_s = [], []
    for l in range(depth):
        mod_p = mod_all[l, :bp][:, None, :]
        mod_s = jnp.repeat(mod_all[l, bp:rows], dec, axis=0)[None]
        xp, sp = _layer(l, xp, mod_p, bp, zero_p, W,
                        lambda q, kb, vb, k_rows, v_rows: _sb_prompt(q, kb, vb, bp), sb_bias[l])
        state_s = (_gla_state_to_bd(state_gla[l]), _rwkv_state_in(state_rwkv[l]), state_shift[l][:, None, :])
        xs, ss = _layer(l, xs, mod_s, bs, state_s, W,
                        lambda q, kb, vb, k_rows, v_rows: _sb_sample(q, k_rows, v_rows, cache_k, cache_v, l,
                                                                     page_table, sb_bias[l], bs))
        st_p.append(sp)
        st_s.append(ss)

    def stack(sts, idx, shape):
        return jnp.stack([s[idx] for s in sts]).reshape((depth,) + shape)

    outs = [xp.reshape(bp, seq, d), xs.reshape(bs, dec, d)]
    for sts, b, t in ((st_p, bp, seq), (st_s, bs, dec)):
        outs += [
            stack(sts, 0, (b, t, SB_HEADS, HEAD_DIM)), stack(sts, 1, (b, t, SB_HEADS, HEAD_DIM)),
            stack(sts, 2, (b, GLA_HEADS, GLA_DK, GLA_DV)), stack(sts, 3, (b, RWKV_HEADS, HEAD_DIM, HEAD_DIM)),
            stack(sts, 4, (b, RWKV_COLS)),
        ]
    return tuple(outs)
```

```python
import functools
import math

import numpy as np
import jax
import jax.numpy as jnp
from jax import lax
from jax.experimental import pallas as pl
from jax.experimental.pallas import tpu as pltpu

F32 = jnp.float32
BF16 = jnp.bfloat16
HIGHEST = lax.Precision.HIGHEST

HEAD_DIM = 64
GLA_HEADS = 4
GLA_DK = 32
GLA_DV = 64
GLA_LORA = 16
GLA_TAU = 16.0
GLA_CHUNK = 16
RWKV_HEADS = 4
RWKV_W_LORA = 64
RWKV_A_LORA = 64
RWKV_G_LORA = 128
RWKV_LN_EPS = 64e-5
SB_HEADS = 8
TOP_K = 2
EPS = 1e-6
LOG2E = math.log2(math.e)
SB_DEAD_BITS = 160.0

GLA_QK = GLA_HEADS * GLA_DK
GLA_V = GLA_HEADS * GLA_DV
GLA_PCOLS = 2 * GLA_QK + 2 * GLA_V + 128
RWKV_W = RWKV_HEADS * HEAD_DIM
RWKV_COLS = 3 * RWKV_W + RWKV_W_LORA + RWKV_A_LORA + RWKV_G_LORA
SB_W = SB_HEADS * HEAD_DIM
SB_COLS = 3 * SB_W
LANES = 128
PAGE = 128
VMEM_LIMIT = 56 * 1024 * 1024


def _cparams(sem):
    return pltpu.CompilerParams(dimension_semantics=sem, vmem_limit_bytes=VMEM_LIMIT)


def _tile(n, pref):
    if n <= pref:
        return n
    t = pref
    while n % t:
        t -= 8
    return t


def _block_ones(n, group, value=1.0):
    idx = np.arange(n) // group
    return jnp.asarray((idx[:, None] == idx[None, :]).astype(np.float32) * value)


def _dot(a, b):
    return jnp.dot(a, b, preferred_element_type=F32)


def _dot_hi(a, b):
    return jnp.dot(a, b, preferred_element_type=F32, precision=HIGHEST)


def _dot_nt(a, b, precision=None):
    return lax.dot_general(a, b, (((1,), (1,)), ((), ())), preferred_element_type=F32, precision=precision)


def _dot_tn(a, b, precision=None):
    return lax.dot_general(a, b, (((0,), (0,)), ((), ())), preferred_element_type=F32, precision=precision)


def _dot_split(x, m_bf16):
    hi = x.astype(BF16)
    lo = (x - hi.astype(F32)).astype(BF16)
    return _dot(hi, m_bf16) + _dot(lo, m_bf16)


def _sigmoid(x):
    return 1.0 / (1.0 + jnp.exp(-x))


def _softplus(x):
    return jnp.maximum(x, 0.0) + jnp.log(1.0 + jnp.exp(-jnp.abs(x)))


def _mod_kernel(c_ref, w_ref, b_ref, o_ref):
    c = c_ref[...]
    o_ref[0] = _dot_hi(c * _sigmoid(c), w_ref[0]) + b_ref[0]


def _modulation(c_all, w_ada, b_ada):
    depth, d, d6 = w_ada.shape
    r = c_all.shape[0]
    tn = 1024
    return pl.pallas_call(
        _mod_kernel,
        grid=(depth, d6 // tn),
        in_specs=[
            pl.BlockSpec((r, d), lambda l, j: (0, 0)),
            pl.BlockSpec((1, d, tn), lambda l, j: (l, 0, j)),
            pl.BlockSpec((1, 1, tn), lambda l, j: (l, 0, j)),
        ],
        out_specs=pl.BlockSpec((1, r, tn), lambda l, j: (l, 0, j)),
        out_shape=jax.ShapeDtypeStruct((depth, r, d6), F32),
        compiler_params=_cparams(("arbitrary", "arbitrary")),
    )(c_all, w_ada, b_ada.reshape(depth, 1, d6))


def _mod_spec(mod, tm, tpb, chunk):
    d = mod.shape[2] // 6
    if mod.shape[1] == 1:
        return pl.BlockSpec((1, 1, d), lambda i, *_: (i // tpb, 0, chunk))
    return pl.BlockSpec((1, tm, d), lambda i, *_: (i // tpb, i % tpb, chunk))


def _inproj_kernel(x_ref, sh_ref, sc_ref, g_ref, w_ref, pg_ref, pr_ref, ps_ref):
    x = x_ref[...]
    ms = jnp.mean(x * x, axis=-1, keepdims=True)
    h = x * lax.rsqrt(ms + EPS) * g_ref[...]
    h = h * (1.0 + sc_ref[0]) + sh_ref[0]
    p = _dot(h.astype(BF16), w_ref[...])
    pg_ref[...] = p[:, :GLA_PCOLS]
    pr_ref[...] = p[:, GLA_PCOLS:GLA_PCOLS + RWKV_COLS]
    ps_ref[...] = p[:, GLA_PCOLS + RWKV_COLS:]


def _inproj(x, mod, norm_g, w):
    n, d = x.shape
    nb = mod.shape[0]
    tm = _tile(n // nb, 256)
    tpb = (n // nb) // tm
    ncols = w.shape[1]
    return pl.pallas_call(
        _inproj_kernel,
        grid=(n // tm,),
        in_specs=[
            pl.BlockSpec((tm, d), lambda i: (i, 0)),
            _mod_spec(mod, tm, tpb, 0),
            _mod_spec(mod, tm, tpb, 1),
            pl.BlockSpec((1, d), lambda i: (0, 0)),
            pl.BlockSpec((d, ncols), lambda i: (0, 0)),
        ],
        out_specs=[
            pl.BlockSpec((tm, GLA_PCOLS), lambda i: (i, 0)),
            pl.BlockSpec((tm, RWKV_COLS), lambda i: (i, 0)),
            pl.BlockSpec((tm, SB_COLS), lambda i: (i, 0)),
        ],
        out_shape=[
            jax.ShapeDtypeStruct((n, GLA_PCOLS), F32),
            jax.ShapeDtypeStruct((n, RWKV_COLS), F32),
            jax.ShapeDtypeStruct((n, SB_COLS), F32),
        ],
        compiler_params=_cparams(("arbitrary",)),
    )(x, mod, mod, norm_g.reshape(1, d), w)


def _gla_kernel(p_ref, s0_ref, wa2_ref, ba_ref, ng_ref, hexp_ref, havg_ref, bd_ref, tril_ref, ones_ref,
                o_ref, s_ref, b_scr, qin_scr, kout_scr, dec_scr, o_scr, *, chunk, n_chunks):
    c_len = chunk

    @pl.when(pl.program_id(1) == 0)
    def _():
        s_ref[0] = s0_ref[0]

    glow = p_ref[0, :, 2 * GLA_QK + 2 * GLA_V:]
    x = _dot_hi(glow, wa2_ref[...]) + ba_ref[...]
    la = -_softplus(-x) * (1.0 / GLA_TAU)
    la_hi = la.astype(BF16)
    la_lo = (la - la_hi.astype(F32)).astype(BF16)
    b = _dot(tril_ref[...], la_hi) + _dot(tril_ref[...], la_lo)
    b_tot = _dot(ones_ref[...], la_hi) + _dot(ones_ref[...], la_lo)
    b_scr[...] = b
    qin_scr[...] = p_ref[0, :, 0:GLA_QK] * (GLA_DK ** -0.5) * jnp.exp(b)
    kout_scr[...] = p_ref[0, :, GLA_QK:2 * GLA_QK] * jnp.exp(b_tot - b)
    dec_scr[...] = jnp.exp(b_tot)
    row = lax.broadcasted_iota(jnp.int32, (c_len, 1), 0)
    hexp = hexp_ref[...]
    bd = bd_ref[...]

    def body(c, carry):
        sl = pl.ds(pl.multiple_of(c * c_len, c_len), c_len)
        q = p_ref[0, sl, 0:GLA_QK] * (GLA_DK ** -0.5)
        k = p_ref[0, sl, GLA_QK:2 * GLA_QK]
        v = p_ref[0, sl, 2 * GLA_QK:2 * GLA_QK + GLA_V]
        b = b_scr[sl, :]
        zs = []
        for s in range(c_len):
            m = row >= s
            e = jnp.exp(jnp.where(m, b - b[s:s + 1, :], 0.0))
            zs.append(jnp.where(m, q * k[s:s + 1, :] * e, 0.0))
        a_exp = _dot_split(jnp.concatenate(zs, axis=0), hexp)
        o = jnp.zeros((c_len, GLA_V), F32)
        for s in range(c_len):
            o = o + a_exp[s * c_len:(s + 1) * c_len, :] * v[s:s + 1, :]
        st = s_ref[0]
        o_scr[sl, :] = o + _dot_nt(qin_scr[sl, :], st, HIGHEST)
        s_ref[0] = st * dec_scr[pl.ds(c * c_len, 1), :] + _dot_tn(v, kout_scr[sl, :], HIGHEST) * bd
        return carry

    lax.fori_loop(0, n_chunks, body, 0, unroll=min(4, n_chunks))
    o = o_scr[...]
    r = p_ref[0, :, 2 * GLA_QK + GLA_V:2 * GLA_QK + 2 * GLA_V]
    on = o * lax.rsqrt(_dot_split(o * o, havg_ref[...]) + EPS) * ng_ref[...]
    o_ref[0] = (on * (r * _sigmoid(r))).astype(o_ref.dtype)


def _gla(p_gla, s0t, wa2p, ba, ng_t, batch, chunk):
    n = p_gla.shape[0]
    t = n // batch
    tt = _tile(t, 512)
    n_chunks = tt // chunk
    p3 = p_gla.reshape(batch, t, GLA_PCOLS)
    hexp = jnp.asarray((np.arange(GLA_QK)[:, None] // GLA_DK == np.arange(GLA_V)[None, :] // GLA_DV).astype(np.float32))
    havg = _block_ones(GLA_V, GLA_DV, 1.0 / GLA_DV)
    bd = jnp.asarray((np.arange(GLA_V)[:, None] // GLA_DV == np.arange(GLA_QK)[None, :] // GLA_DK).astype(np.float32))
    rows = np.arange(tt)
    same = rows[:, None] // chunk == rows[None, :] // chunk
    tril = jnp.asarray((same & (rows[:, None] >= rows[None, :])).astype(np.float32)).astype(BF16)
    ones = jnp.asarray(same.astype(np.float32)).astype(BF16)
    const = lambda shape: pl.BlockSpec(shape, lambda b, i: (0,) * len(shape))
    scr = lambda w: pltpu.VMEM((tt, w), F32)
    o, s = pl.pallas_call(
        functools.partial(_gla_kernel, chunk=chunk, n_chunks=n_chunks),
        grid=(batch, t // tt),
        in_specs=[
            pl.BlockSpec((1, tt, GLA_PCOLS), lambda b, i: (b, i, 0)),
            pl.BlockSpec((1, GLA_V, GLA_QK), lambda b, i: (b, 0, 0)),
            const((LANES, GLA_QK)), const((1, GLA_QK)), const((1, GLA_V)),
            const((GLA_QK, GLA_V)), const((GLA_V, GLA_V)), const((GLA_V, GLA_QK)), const((tt, tt)), const((tt, tt)),
        ],
        out_specs=[
            pl.BlockSpec((1, tt, GLA_V), lambda b, i: (b, i, 0)),
            pl.BlockSpec((1, GLA_V, GLA_QK), lambda b, i: (b, 0, 0)),
        ],
        out_shape=[
            jax.ShapeDtypeStruct((batch, t, GLA_V), BF16),
            jax.ShapeDtypeStruct((batch, GLA_V, GLA_QK), F32),
        ],
        scratch_shapes=[scr(GLA_QK), scr(GLA_QK), scr(GLA_QK), scr(GLA_QK), scr(GLA_V)],
        compiler_params=_cparams(("arbitrary", "arbitrary")),
    )(p3, s0t, wa2p, ba, ng_t, hexp.astype(BF16), havg.astype(BF16), bd, tril, ones)
    return o.reshape(n, GLA_V), s


def _gla_state_to_bd(s0):
    b = s0.shape[0]
    eye = jnp.eye(GLA_HEADS, dtype=s0.dtype)
    st = jnp.einsum('bhkv,hg->bhvgk', s0, eye)
    return st.reshape(b, GLA_V, GLA_QK)


def _gla_state_from_bd(st):
    b = st.shape[0]
    s5 = st.reshape(b, GLA_HEADS, GLA_DV, GLA_HEADS, GLA_DK)
    diag = jnp.stack([s5[:, h, :, h, :] for h in range(GLA_HEADS)], axis=1)
    return jnp.swapaxes(diag, 2, 3)


def _rwkv_kernel(p_ref, sh0_ref, s0_ref, mu_ref, w0_ref, ww2_ref, a0_ref, wa2_ref, wg2_ref, kk_ref, ka_ref,
                 rk_ref, lng_ref, lnb_ref, ones_ref, onesb_ref, avg_ref, sel_ref,
                 o_ref, s_ref, prev_scr, nkk_scr, d_scr, b_scr, k_scr, r_scr, v_scr, y_scr, *, nbb, tt):
    first = pl.program_id(1) == 0

    @pl.when(first)
    def _():
        s_ref[...] = s0_ref[...]
        prev_scr[...] = sh0_ref[...]

    ones_f = ones_ref[...]
    ones_b = onesb_ref[...]
    avg = avg_ref[...]
    sel = sel_ref[...]
    row = lax.broadcasted_iota(jnp.int32, (tt, 1), 0)
    gate = []
    for bb in range(nbb):
        p = p_ref[bb]
        p_prev = jnp.where(row == 0, prev_scr[bb], pltpu.roll(p, 1, axis=0))
        prev_scr[bb] = p[tt - 1:tt, :]
        xm = p + (p_prev - p) * mu_ref[...]
        r = xm[:, 0:RWKV_W]
        k = xm[:, RWKV_W:2 * RWKV_W]
        v = xm[:, 2 * RWKV_W:3 * RWKV_W]
        xwa = xm[:, 3 * RWKV_W:3 * RWKV_W + LANES]
        xg = xm[:, 3 * RWKV_W + LANES:]
        w = -_softplus(-(w0_ref[...] + _dot_hi(jnp.tanh(xwa), ww2_ref[...]))) - 0.5
        a = _sigmoid(a0_ref[...] + _dot_hi(xwa, wa2_ref[...]))
        gate.append(_dot_hi(_sigmoid(xg), wg2_ref[...]))
        kk = k * kk_ref[...]
        nrm = jnp.sqrt(_dot_hi(kk * kk, ones_f))
        kk = kk / jnp.maximum(nrm, 1e-12)
        k2 = k * (1.0 + (a - 1.0) * ka_ref[...])
        nkk_scr[bb] = -kk
        d_scr[bb] = jnp.exp(-jnp.exp(w))
        b_scr[bb] = kk * a
        k_scr[bb] = k2
        r_scr[bb] = r
        v_scr[bb] = v

    def head_sums(xs):
        stacked = jnp.concatenate([x[:, half * LANES:(half + 1) * LANES] for x in xs for half in range(2)], axis=0)
        res = _dot_split(stacked, ones_b)
        out, row = [], 0
        for x in xs:
            n_rows = x.shape[0]
            out.append(jnp.concatenate([res[row:row + n_rows], res[row + n_rows:row + 2 * n_rows]], axis=1))
            row += 2 * n_rows
        return out

    def read_out(s, bb, ts):
        yb, = head_sums([s * jnp.broadcast_to(r_scr[bb, ts, :], (HEAD_DIM, RWKV_W))])
        y_scr[bb, ts, :] = jnp.sum(yb * sel, axis=0, keepdims=True)

    def step(t, states):
        ts = pl.ds(t, 1)
        tp = pl.ds(jnp.maximum(t - 1, 0), 1)
        bc = lambda ref, bb, rows: jnp.broadcast_to(ref[bb, rows, :], (HEAD_DIM, RWKV_W))
        sums = head_sums([jnp.concatenate([states[bb] * bc(nkk_scr, bb, ts), states[bb] * bc(r_scr, bb, tp),
                                           sel * bc(v_scr, bb, ts)], axis=0) for bb in range(nbb)])
        new = []
        for bb in range(nbb):
            res = sums[bb]
            sa = res[0:HEAD_DIM]
            y_scr[bb, tp, :] = jnp.sum(res[HEAD_DIM:2 * HEAD_DIM] * sel, axis=0, keepdims=True)
            vb = res[2 * HEAD_DIM:]
            new.append(states[bb] * bc(d_scr, bb, ts) + sa * bc(b_scr, bb, ts) + vb * bc(k_scr, bb, ts))
        return tuple(new)

    states = lax.fori_loop(0, tt, step, tuple(s_ref[bb] for bb in range(nbb)))
    for bb in range(nbb):
        s_ref[bb] = states[bb]
        read_out(states[bb], bb, pl.ds(tt - 1, 1))
        y = y_scr[bb]
        mean = _dot_hi(y, avg)
        yc = y - mean
        var = _dot_hi(yc * yc, avg)
        yn = yc * lax.rsqrt(var + RWKV_LN_EPS) * lng_ref[...] + lnb_ref[...]
        bonus = _dot_hi(r_scr[bb] * k_scr[bb] * rk_ref[...], ones_f) * v_scr[bb]
        o_ref[bb] = ((yn + bonus) * gate[bb]).astype(o_ref.dtype)


def _rwkv(p_rwkv, shift0, s0, wl, batch, nbb):
    n = p_rwkv.shape[0]
    t = n // batch
    tt = _tile(t, 256)
    p3 = p_rwkv.reshape(batch, t, RWKV_COLS)
    ones_f = _block_ones(RWKV_W, HEAD_DIM)
    avg = _block_ones(RWKV_W, HEAD_DIM, 1.0 / HEAD_DIM)
    sel = jnp.asarray((np.arange(HEAD_DIM)[:, None] == np.arange(RWKV_W)[None, :] % HEAD_DIM).astype(np.float32))
    const = lambda shape: pl.BlockSpec(shape, lambda b, i: (0,) * len(shape))
    row = const((1, RWKV_W))
    scr = lambda: pltpu.VMEM((nbb, tt, RWKV_W), F32)
    o, s = pl.pallas_call(
        functools.partial(_rwkv_kernel, nbb=nbb, tt=tt),
        grid=(batch // nbb, t // tt),
        in_specs=[
            pl.BlockSpec((nbb, tt, RWKV_COLS), lambda b, i: (b, i, 0)),
            pl.BlockSpec((nbb, 1, RWKV_COLS), lambda b, i: (b, 0, 0)),
            pl.BlockSpec((nbb, HEAD_DIM, RWKV_W), lambda b, i: (b, 0, 0)),
            const((1, RWKV_COLS)), row, const((LANES, RWKV_W)), row, const((LANES, RWKV_W)),
            const((RWKV_G_LORA, RWKV_W)), row, row, row, row, row,
            const((RWKV_W, RWKV_W)), const((LANES, LANES)), const((RWKV_W, RWKV_W)), const((HEAD_DIM, RWKV_W)),
        ],
        out_specs=[
            pl.BlockSpec((nbb, tt, RWKV_W), lambda b, i: (b, i, 0)),
            pl.BlockSpec((nbb, HEAD_DIM, RWKV_W), lambda b, i: (b, 0, 0)),
        ],
        out_shape=[
            jax.ShapeDtypeStruct((batch, t, RWKV_W), BF16),
            jax.ShapeDtypeStruct((batch, HEAD_DIM, RWKV_W), F32),
        ],
        scratch_shapes=[pltpu.VMEM((nbb, 1, RWKV_COLS), F32)] + [scr() for _ in range(7)],
        compiler_params=_cparams(("arbitrary", "arbitrary")),
    )(p3, shift0, s0, wl['mu'], wl['w0'], wl['ww2'], wl['a0'], wl['wa2'], wl['wg2'], wl['kk'], wl['ka'],
      wl['rk'], wl['lng'], wl['lnb'], ones_f, _block_ones(LANES, HEAD_DIM).astype(BF16), avg, sel)
    return o.reshape(n, RWKV_W), s


def _rwkv_state_in(s0):
    b = s0.shape[0]
    return jnp.transpose(s0, (0, 2, 1, 3)).reshape(b, HEAD_DIM, RWKV_W)


def _rwkv_state_out(s):
    b = s.shape[0]
    return jnp.transpose(s.reshape(b, HEAD_DIM, RWKV_HEADS, HEAD_DIM), (0, 2, 1, 3))


def _sbprep_kernel(p_ref, qg_ref, kg_ref, avg_ref, *rest, aug):
    if aug:
        qrow_ref, krow_ref, q_ref, k_ref, kb_ref, vb_ref = rest
    else:
        q_ref, k_ref, kb_ref, vb_ref = rest
    avg = avg_ref[...]
    q = p_ref[:, 0:SB_W]
    k = p_ref[:, SB_W:2 * SB_W]
    qn = q * lax.rsqrt(_dot_hi(q * q, avg) + EPS) * (qg_ref[...] * (HEAD_DIM ** -0.5 * LOG2E))
    kn = k * lax.rsqrt(_dot_hi(k * k, avg) + EPS) * kg_ref[...]
    k_ref[...] = kn
    vb_ref[...] = p_ref[:, 2 * SB_W:].astype(BF16)
    if aug:
        lo = lax.broadcasted_iota(jnp.int32, (1, LANES), 1) < HEAD_DIM

        def widen(x, row):
            parts = []
            for h in range(SB_HEADS):
                blk = x[:, (h // 2) * LANES:(h // 2 + 1) * LANES]
                if h % 2:
                    blk = pltpu.roll(blk, HEAD_DIM, axis=1)
                parts.append(jnp.where(lo, blk, 0.0))
            return (jnp.concatenate(parts, axis=1) + row).astype(BF16)

        q_ref[...] = widen(qn, qrow_ref[...])
        kb_ref[...] = widen(kn, krow_ref[...])
    else:
        q_ref[...] = qn.astype(BF16)
        kb_ref[...] = kn.astype(BF16)


def _sbprep(p_sb, qg_t, kg_t, bias=None):
    n = p_sb.shape[0]
    tm = _tile(n, 512)
    aug = bias is not None
    avg = _block_ones(SB_W, HEAD_DIM, 1.0 / HEAD_DIM)
    wide = SB_HEADS * LANES if aug else SB_W
    out = lambda w: pl.BlockSpec((tm, w), lambda i: (i, 0))
    const = lambda shape: pl.BlockSpec(shape, lambda i: (0,) * len(shape))
    in_specs = [pl.BlockSpec((tm, SB_COLS), lambda i: (i, 0)), const((1, SB_W)), const((1, SB_W)), const((SB_W, SB_W))]
    args = [p_sb, qg_t, kg_t, avg]
    if aug:
        b2 = bias * LOG2E
        hi = b2.astype(BF16).astype(F32)
        lo = (b2 - hi).astype(BF16).astype(F32)
        qrow = jnp.zeros((SB_HEADS, LANES), F32).at[:, HEAD_DIM].set(hi).at[:, HEAD_DIM + 1].set(lo)
        krow = jnp.zeros((SB_HEADS, LANES), F32).at[:, HEAD_DIM:HEAD_DIM + 2].set(1.0)
        in_specs += [const((1, wide)), const((1, wide))]
        args += [qrow.reshape(1, wide), krow.reshape(1, wide)]
    return pl.pallas_call(
        functools.partial(_sbprep_kernel, aug=aug),
        grid=(n // tm,),
        in_specs=in_specs,
        out_specs=[out(wide), out(SB_W), out(wide), out(SB_W)],
        out_shape=[
            jax.ShapeDtypeStruct((n, wide), BF16),
            jax.ShapeDtypeStruct((n, SB_W), F32),
            jax.ShapeDtypeStruct((n, wide), BF16),
            jax.ShapeDtypeStruct((n, SB_W), BF16),
        ],
        compiler_params=_cparams(("arbitrary",)),
    )(*args)


def _sb_keep(z, mask):
    keep = jnp.maximum(z, 0.0) + jnp.log2(1.0 + jnp.exp2(-jnp.abs(z)))
    if mask is not None:
        keep = jnp.where(mask, keep, 0.0)
    return z - keep, keep.astype(BF16)


def _sb_logits_stage(z, mask, cum):
    zk, kb = _sb_keep(z, mask)
    tk = z.shape[1]
    cs = _dot(kb, cum)
    return zk - cs[:, :tk], cs[:, tk:]


def _sb_weights(part, r_run, mask):
    reps = part.shape[1] // r_run.shape[1]
    e = jnp.exp2(part - (jnp.concatenate([r_run] * reps, axis=1) if reps > 1 else r_run))
    if mask is not None:
        e = jnp.where(mask, e, 0.0)
    return e


def _sb_prompt_kernel(q_ref, k_ref, v_ref, cum_ref, o_ref, zk_scr, kb_scr, *, tq):
    i = pl.program_id(2)
    cum = cum_ref[...]
    rows = lax.broadcasted_iota(jnp.int32, (tq, tq), 0)
    cols = lax.broadcasted_iota(jnp.int32, (tq, tq), 1)
    diag = cols < rows

    def stage1(j, mask):
        ks = pl.ds(pl.multiple_of(j * tq, tq), tq)
        return [_sb_keep(_dot_nt(q_ref[:, u * LANES:(u + 1) * LANES], k_ref[ks, u * LANES:(u + 1) * LANES]), mask)
                for u in range(2)]

    def stage2(j, pend, carry, mask, valid):
        vb = v_ref[pl.ds(pl.multiple_of(j * tq, tq), tq), :]
        out = []
        for u in range(2):
            zk, kb = pend[u]
            acc, r_run = carry[u]
            cs = _dot(kb, cum)
            tot = cs[:, tq:]
            pv = _dot(_sb_weights(zk - cs[:, :tq], r_run, mask).astype(BF16), vb)
            if valid is not None:
                pv = jnp.where(valid, pv, 0.0)
                tot = jnp.where(valid, tot, 0.0)
            out.append((acc + pv, r_run + tot))
        return tuple(out)

    carry = tuple((jnp.zeros((tq, LANES), F32), jnp.zeros((tq, LANES), F32)) for _ in range(2))
    carry = stage2(i, stage1(i, diag), carry, diag, None)
    zk_scr[...] = jnp.zeros_like(zk_scr)
    kb_scr[...] = jnp.zeros_like(kb_scr)

    def body(state):
        n, _, carry = state
        live = jnp.minimum(jnp.min(carry[0][1]), jnp.min(carry[1][1])) < SB_DEAD_BITS
        pend = [(zk_scr[u], kb_scr[u]) for u in range(2)]
        carry = stage2(jnp.maximum(i - n, 0), pend, carry, None, n >= 1)
        nxt = stage1(jnp.maximum(i - 1 - n, 0), None)
        for u in range(2):
            zk_scr[u] = nxt[u][0]
            kb_scr[u] = nxt[u][1]
        return n + 1, live, carry

    _, _, carry = lax.while_loop(lambda s: (s[0] < i + 1) & s[1], body, (jnp.int32(0), jnp.bool_(True), carry))
    lo = lax.broadcasted_iota(jnp.int32, (1, LANES), 1) < HEAD_DIM
    o_ref[...] = jnp.where(lo, carry[0][0], carry[1][0]).astype(o_ref.dtype)


def _cum_matrix(tk):
    j = np.arange(tk)
    later = (j[:, None] > j[None, :]).astype(np.float32)
    return jnp.asarray(np.concatenate([later, np.ones((tk, LANES), np.float32)], axis=1)).astype(BF16)


def _sb_prompt(q_aug, k_aug, vb, batch):
    n = vb.shape[0]
    t = n // batch
    tq = _tile(t, 256)
    tpb = t // tq
    return pl.pallas_call(
        functools.partial(_sb_prompt_kernel, tq=tq),
        grid=(batch, SB_HEADS // 2, tpb),
        in_specs=[
            pl.BlockSpec((tq, 2 * LANES), lambda b, h, i: (b * tpb + i, h)),
            pl.BlockSpec((t, 2 * LANES), lambda b, h, i: (b, h)),
            pl.BlockSpec((t, LANES), lambda b, h, i: (b, h)),
            pl.BlockSpec((tq, tq + LANES), lambda b, h, i: (0, 0)),
        ],
        out_specs=pl.BlockSpec((tq, LANES), lambda b, h, i: (b * tpb + i, h)),
        out_shape=jax.ShapeDtypeStruct((n, SB_W), BF16),
        scratch_shapes=[pltpu.VMEM((2, tq, tq), F32), pltpu.VMEM((2, tq, tq), BF16)],
        compiler_params=_cparams(("arbitrary", "arbitrary", "arbitrary")),
    )(q_aug, k_aug, vb, _cum_matrix(tq))


def _sb_sample_kernel(pt_ref, q_ref, kn_ref, vn_ref, bias_ref, bd_ref, cum_ref, *rest, n_pages, ppb, dec):
    k_refs = rest[:ppb]
    v_refs = rest[ppb:2 * ppb]
    o_ref, qbd_scr, acc_scr, run_scr = rest[2 * ppb:]
    g = pl.program_id(1)
    rows_n = SB_HEADS * dec
    bd = bd_ref[...]
    cum = cum_ref[...]
    bias = bias_ref[...]

    def logits(kt, mask):
        return _sb_logits_stage(_dot(qbd_scr[...], kt.astype(BF16)) + bias, mask, cum)

    def attend(kt, vt, mask):
        part, tot = logits(kt, mask)
        acc_scr[...] += _dot_nt(_sb_weights(part, run_scr[...], mask).astype(BF16), vt.astype(BF16))
        run_scr[...] += tot

    @pl.when(g == 0)
    def _():
        q = q_ref[0].astype(F32)
        qbd_scr[...] = (jnp.concatenate([q] * SB_HEADS, axis=0) * bd).astype(BF16)
        acc_scr[...] = jnp.zeros_like(acc_scr)
        run_scr[...] = jnp.zeros_like(run_scr)
        qi = lax.broadcasted_iota(jnp.int32, (rows_n, PAGE), 0) % dec
        kj = lax.broadcasted_iota(jnp.int32, (rows_n, PAGE), 1)
        attend(kn_ref[0], vn_ref[0], kj < qi)

    pend = [logits(k_refs[pidx][0, 0].reshape(SB_W, PAGE), None) for pidx in range(ppb)]
    r_run = run_scr[...]
    acc = acc_scr[...]
    for pidx in range(ppb):
        part, tot = pend[pidx]
        vt = v_refs[pidx][0, 0].reshape(SB_W, PAGE).astype(BF16)
        acc = acc + _dot_nt(_sb_weights(part, r_run, None).astype(BF16), vt)
        r_run = r_run + tot
    run_scr[...] = r_run
    acc_scr[...] = acc

    @pl.when(g == n_pages // ppb - 1)
    def _():
        masked = acc_scr[...] * bd
        o = masked[0:dec, :]
        for h in range(1, SB_HEADS):
            o = o + masked[h * dec:(h + 1) * dec, :]
        o_ref[0] = o.astype(o_ref.dtype)


def _sb_sample(qs, k_rows, v_rows, cache_k, cache_v, layer, page_table, bias, batch):
    n = qs.shape[0]
    dec = n // batch
    n_pages = page_table.shape[1]
    ppb = 8 if n_pages % 8 == 0 else 1
    rows_n = SB_HEADS * dec
    q3 = qs.reshape(batch, dec, SB_W)

    def as_page(a):
        a = jnp.swapaxes(a.reshape(batch, dec, SB_W), 1, 2)
        return jnp.pad(a, ((0, 0), (0, 0), (0, PAGE - dec)))

    bias_rows = jnp.broadcast_to(jnp.repeat(bias * LOG2E, dec)[:, None], (rows_n, PAGE))
    bd = jnp.asarray((np.arange(rows_n)[:, None] // dec == np.arange(SB_W)[None, :] // HEAD_DIM).astype(np.float32))

    def page_spec(pidx):
        return pl.BlockSpec((1, 1, SB_HEADS, HEAD_DIM, PAGE),
                            lambda b, g, pt: (layer, pt[b, n_pages - 1 - (g * ppb + pidx)], 0, 0, 0))

    const = lambda shape: pl.BlockSpec(shape, lambda b, g, pt: (0,) * len(shape))
    grid_spec = pltpu.PrefetchScalarGridSpec(
        num_scalar_prefetch=1,
        grid=(batch, n_pages // ppb),
        in_specs=[
            pl.BlockSpec((1, dec, SB_W), lambda b, g, pt: (b, 0, 0)),
            pl.BlockSpec((1, SB_W, PAGE), lambda b, g, pt: (b, 0, 0)),
            pl.BlockSpec((1, SB_W, PAGE), lambda b, g, pt: (b, 0, 0)),
            const((rows_n, PAGE)), const((rows_n, SB_W)), const((PAGE, PAGE + LANES)),
        ] + [page_spec(pidx) for pidx in range(ppb)] * 2,
        out_specs=pl.BlockSpec((1, dec, SB_W), lambda b, g, pt: (b, 0, 0)),
        scratch_shapes=[
            pltpu.VMEM((rows_n, SB_W), BF16),
            pltpu.VMEM((rows_n, SB_W), F32),
            pltpu.VMEM((rows_n, PAGE), F32),
        ],
    )
    o = pl.pallas_call(
        functools.partial(_sb_sample_kernel, n_pages=n_pages, ppb=ppb, dec=dec),
        grid_spec=grid_spec,
        out_shape=jax.ShapeDtypeStruct((batch, dec, SB_W), BF16),
        compiler_params=_cparams(("arbitrary", "arbitrary")),
    )(page_table, q3, as_page(k_rows), as_page(v_rows), bias_rows, bd, _cum_matrix(PAGE),
      *([cache_k] * ppb), *([cache_v] * ppb))
    return o.reshape(n, SB_W)


def _outproj_kernel(og_ref, or_ref, os_ref, x_ref, g1_ref, sh_ref, sc_ref, ng_ref, wg_ref, wr_ref, ws_ref,
                    *rest, with_router):
    if with_router:
        router_ref, x1_ref, h2_ref, gates_ref = rest
    else:
        x1_ref, h2_ref = rest
    mix = _dot(og_ref[...], wg_ref[...]) + _dot(or_ref[...], wr_ref[...]) + _dot(os_ref[...], ws_ref[...])
    x1 = x_ref[...] + g1_ref[0] * mix
    x1_ref[...] = x1
    ms = jnp.mean(x1 * x1, axis=-1, keepdims=True)
    h2 = x1 * lax.rsqrt(ms + EPS) * ng_ref[...]
    h2 = h2 * (1.0 + sc_ref[0]) + sh_ref[0]
    h2_ref[...] = h2.astype(BF16)
    if with_router:
        n_exp = 8
        logits = _dot_hi(h2, router_ref[...])
        lane = lax.broadcasted_iota(jnp.int32, logits.shape, 1).astype(F32)
        neg = jnp.float32(-jnp.inf)
        lg = jnp.where(lane < n_exp, logits, neg)
        m1 = jnp.max(lg, axis=1, keepdims=True)
        i1 = jnp.min(jnp.where(lg == m1, lane, float(LANES)), axis=1, keepdims=True)
        lg2 = jnp.where(lane == i1, neg, lg)
        m2 = jnp.max(lg2, axis=1, keepdims=True)
        i2 = jnp.min(jnp.where(lg2 == m2, lane, float(LANES)), axis=1, keepdims=True)
        e2 = jnp.exp(m2 - m1)
        gates_ref[...] = jnp.where(lane == i1, 1.0 / (1.0 + e2), 0.0) + jnp.where(lane == i2, e2 / (1.0 + e2), 0.0)


def _outproj(o_gla, o_rwkv, o_sb, x, mod, norm_g, wg, wr, ws, router):
    n, d = x.shape
    nb = mod.shape[0]
    tm = _tile(n // nb, 512)
    tpb = (n // nb) // tm
    with_router = router is not None
    row = lambda w: pl.BlockSpec((tm, w), lambda i: (i, 0))
    const = lambda shape: pl.BlockSpec(shape, lambda i: (0,) * len(shape))
    in_specs = [row(GLA_V), row(RWKV_W), row(SB_W), row(d),
                _mod_spec(mod, tm, tpb, 2), _mod_spec(mod, tm, tpb, 3), _mod_spec(mod, tm, tpb, 4),
                const((1, d)), const((GLA_V, d)), const((RWKV_W, d)), const((SB_W, d))]
    args = [o_gla, o_rwkv, o_sb, x, mod, mod, mod, norm_g.reshape(1, d), wg, wr, ws]
    out_specs = [row(d), row(d)]
    out_shape = [jax.ShapeDtypeStruct((n, d), F32), jax.ShapeDtypeStruct((n, d), BF16)]
    if with_router:
        in_specs.append(const((d, LANES)))
        args.append(router)
        out_specs.append(row(LANES))
        out_shape.append(jax.ShapeDtypeStruct((n, LANES), F32))
    return pl.pallas_call(
        functools.partial(_outproj_kernel, with_router=with_router),
        grid=(n // tm,),
        in_specs=in_specs, out_specs=out_specs, out_shape=out_shape,
        compiler_params=_cparams(("arbitrary",)),
    )(*args)


def _ffn_kernel(h_ref, x1_ref, g2_ref, *rest, gated):
    if gated:
        gates_ref, w1_ref, w3_ref, w2_ref, o_ref, acc_scr = rest
    else:
        w1_ref, w3_ref, w2_ref, o_ref, acc_scr = rest
    e = pl.program_id(1)
    f = pl.program_id(2)

    @pl.when((e == 0) & (f == 0))
    def _():
        acc_scr[...] = jnp.zeros_like(acc_scr)

    h = h_ref[...]
    a = _dot(h, w1_ref[0])
    t = a * _sigmoid(a) * _dot(h, w3_ref[0])
    if gated:
        gates = gates_ref[...]
        lane = lax.broadcasted_iota(jnp.int32, gates.shape, 1)
        t = t * jnp.sum(jnp.where(lane == e, gates, 0.0), axis=1, keepdims=True)
    acc_scr[...] += _dot(t.astype(BF16), w2_ref[0])

    @pl.when((e == pl.num_programs(1) - 1) & (f == pl.num_programs(2) - 1))
    def _():
        o_ref[...] = x1_ref[...] + g2_ref[0] * acc_scr[...]


def _ffn(h2, x1, mod, w1, w3, w2, gates, tf):
    n, d = x1.shape
    n_exp, _, ff = w1.shape
    nb = mod.shape[0]
    tm = _tile(n // nb, 512)
    tpb = (n // nb) // tm
    gated = gates is not None
    row = lambda w: pl.BlockSpec((tm, w), lambda i, e, f: (i, 0))
    in_specs = [row(d), row(d), _mod_spec(mod, tm, tpb, 5)]
    args = [h2, x1, mod]
    if gated:
        in_specs.append(row(LANES))
        args.append(gates)
    in_specs += [
        pl.BlockSpec((1, d, tf), lambda i, e, f: (e, 0, f)),
        pl.BlockSpec((1, d, tf), lambda i, e, f: (e, 0, f)),
        pl.BlockSpec((1, tf, d), lambda i, e, f: (e, f, 0)),
    ]
    args += [w1, w3, w2]
    return pl.pallas_call(
        functools.partial(_ffn_kernel, gated=gated),
        grid=(n // tm, n_exp, ff // tf),
        in_specs=in_specs,
        out_specs=row(d),
        out_shape=jax.ShapeDtypeStruct((n, d), F32),
        scratch_shapes=[pltpu.VMEM((tm, d), F32)],
        compiler_params=_cparams(("arbitrary", "arbitrary", "arbitrary")),
    )(*args)


def _moe_kernel(cnt_ref, h_ref, x1_ref, g2_ref, gates_ref, pos_ref, w1_ref, w3_ref, w2_ref, o_ref,
                hc_scr, accc_scr, *, tm, ch):
    i = pl.program_id(0)
    e = pl.program_id(1)
    f = pl.program_id(2)
    last_f = f == pl.num_programs(2) - 1
    count = cnt_ref[i, e]
    rank = lax.broadcasted_iota(jnp.int32, (ch, 1), 0)

    def one_hot(c):
        return jnp.where(pos_ref[0, pl.ds(e, 1), :] == rank + c * ch, 1.0, 0.0).astype(BF16)

    @pl.when((e == 0) & (f == 0))
    def _():
        o_ref[...] = jnp.zeros_like(o_ref)

    for c in range(-(-tm // ch)):
        rows = pl.ds(c * ch, ch)

        @pl.when(c * ch < count)
        def _():
            @pl.when(f == 0)
            def _():
                hc_scr[rows, :] = _dot(one_hot(c), h_ref[...]).astype(BF16)
                accc_scr[rows, :] = jnp.zeros((ch, accc_scr.shape[1]), F32)

            hc = hc_scr[rows, :]
            a = _dot(hc, w1_ref[0])
            t = a * _sigmoid(a) * _dot(hc, w3_ref[0])
            accc_scr[rows, :] += _dot(t.astype(BF16), w2_ref[0])

            @pl.when(last_f)
            def _():
                y = accc_scr[rows, :]
                hi = y.astype(BF16)
                lo = (y - hi.astype(F32)).astype(BF16)
                p = one_hot(c)
                gates = gates_ref[...]
                lane = lax.broadcasted_iota(jnp.int32, gates.shape, 1)
                ge = jnp.sum(jnp.where(lane == e, gates, 0.0), axis=1, keepdims=True)
                o_ref[...] += ge * (_dot_tn(p, hi) + _dot_tn(p, lo))

    @pl.when((e == pl.num_programs(1) - 1) & last_f)
    def _():
        o_ref[...] = x1_ref[...] + g2_ref[0] * o_ref[...]


def _moe(h2, x1, mod, w1, w3, w2, gates, tf):
    n, d = x1.shape
    n_exp, _, ff = w1.shape
    nb = mod.shape[0]
    tm = _tile(n // nb, 1024)
    tpb = (n // nb) // tm
    ch = -(-int(tm * TOP_K / n_exp * 1.125) // 32) * 32 if tm >= 512 else tm // 2
    slots = -(-tm // ch)
    n_tiles = n // tm
    routed = (gates[:, :n_exp] > 0.0).reshape(n_tiles, tm, n_exp)
    counts = jnp.sum(routed, axis=1).astype(jnp.int32)
    ranks = jnp.cumsum(routed.astype(jnp.int32), axis=1) - 1
    pos = jnp.swapaxes(jnp.where(routed, ranks, -1), 1, 2)
    row = lambda w: pl.BlockSpec((tm, w), lambda i, e, f, cnt: (i, 0))
    grid_spec = pltpu.PrefetchScalarGridSpec(
        num_scalar_prefetch=1,
        grid=(n_tiles, n_exp, ff // tf),
        in_specs=[
            row(d), row(d), _mod_spec(mod, tm, tpb, 5), row(LANES),
            pl.BlockSpec((1, n_exp, tm), lambda i, e, f, cnt: (i, 0, 0)),
            pl.BlockSpec((1, d, tf), lambda i, e, f, cnt: (e, 0, f)),
            pl.BlockSpec((1, d, tf), lambda i, e, f, cnt: (e, 0, f)),
            pl.BlockSpec((1, tf, d), lambda i, e, f, cnt: (e, f, 0)),
        ],
        out_specs=row(d),
        scratch_shapes=[pltpu.VMEM((slots * ch, d), BF16), pltpu.VMEM((slots * ch, d), F32)],
    )
    return pl.pallas_call(
        functools.partial(_moe_kernel, tm=tm, ch=ch),
        grid_spec=grid_spec,
        out_shape=jax.ShapeDtypeStruct((n, d), F32),
        compiler_params=_cparams(("arbitrary", "arbitrary", "arbitrary")),
    )(counts, h2, x1, mod, gates, pos, w1, w3, w2)


def _pad_rows(w, rows, offset):
    return jnp.pad(w, ((0, 0), (offset, rows - offset - w.shape[1]), (0, 0)))


def _layer(l, x, mod, batch, state, W, sb_fn, sb_bias=None):
    n = x.shape[0]
    gla0, rwkv0, shift0 = state
    p_gla, p_rwkv, p_sb = _inproj(x, mod, W['norm1_g'][l], W['w_in'][l])
    chunk = math.gcd(n // batch, GLA_CHUNK)
    o_gla, gla_st = _gla(p_gla, gla0, W['gla_wa2'][l], W['gla_ba'][l], W['gla_ng'][l], batch, chunk)
    o_rwkv, rwkv_st = _rwkv(p_rwkv, shift0, rwkv0, {k: v[l] for k, v in W['rwkv'].items()}, batch, 2)
    qs, k_rows, kb, vb = _sbprep(p_sb, W['sb_qg'][l], W['sb_kg'][l], sb_bias)
    v_rows = p_sb[:, 2 * SB_W:]
    o_sb = sb_fn(qs, kb, vb, k_rows, v_rows)
    router = W['router'][l // 2] if l % 2 else None
    outs = _outproj(o_gla, o_rwkv, o_sb, x, mod, W['norm2_g'][l], W['wo_g'][l], W['wo_r'][l], W['wo_s'][l], router)
    if l % 2 == 0:
        x1, h2 = outs
        x = _ffn(h2, x1, mod, W['ffn_w1'][l // 2][None], W['ffn_w3'][l // 2][None], W['ffn_w2'][l // 2][None],
                 None, W['ffn_w1'].shape[-1] // 2)
    else:
        x1, h2, gates = outs
        x = _moe(h2, x1, mod, W['moe_w1'][l // 2], W['moe_w3'][l // 2], W['moe_w2'][l // 2], gates,
                 W['moe_w1'].shape[-1] // 7)
    t = n // batch
    shift_row = p_rwkv.reshape(batch, t, RWKV_COLS)[:, -1]
    return x, (k_rows, v_rows, _gla_state_from_bd(gla_st), _rwkv_state_out(rwkv_st), shift_row)


def kernel(x_prompt, x_sample, cache_sb_k, cache_sb_v, state_gla, state_rwkv, state_shift, page_table, c_prompt, c_sample, norm1_g, norm2_g, w_ada, b_ada, w_in, w_out, gla_w_a2, gla_b_a, gla_norm_g, rwkv_mu, rwkv_w0, rwkv_w_w2, rwkv_a0, rwkv_w_a2, rwkv_w_g2, rwkv_k_k, rwkv_k_a, rwkv_r_k, rwkv_ln_g, rwkv_ln_b, sb_q_g, sb_k_g, sb_bias, ffn_w1, ffn_w3, ffn_w2, moe_router, moe_w1, moe_w3, moe_w2):
    depth = w_in.shape[0]
    bp, seq, d = x_prompt.shape
    bs, dec, _ = x_sample.shape
    n_pool, page = cache_sb_k.shape[1:3]

    g0 = 0
    gq, gk, gv = w_in[:, :, g0:g0 + GLA_QK], w_in[:, :, g0 + GLA_QK:g0 + 2 * GLA_QK], w_in[:, :, g0 + 2 * GLA_QK:g0 + 2 * GLA_QK + GLA_V]
    g_low = w_in[:, :, g0 + 2 * GLA_QK + GLA_V:g0 + 2 * GLA_QK + GLA_V + GLA_LORA]
    g_r = w_in[:, :, g0 + 2 * GLA_QK + GLA_V + GLA_LORA:g0 + 2 * GLA_QK + 2 * GLA_V + GLA_LORA]
    gla_cols = 2 * GLA_QK + 2 * GLA_V + GLA_LORA
    g_low = jnp.pad(g_low, ((0, 0), (0, 0), (0, LANES - GLA_LORA)))
    w_in_r = jnp.concatenate([gq, gk, gv, g_r, g_low, w_in[:, :, gla_cols:]], axis=-1).astype(BF16)

    tile_heads = lambda g, h: jnp.tile(g, (1, h))
    W = dict(
        norm1_g=norm1_g, norm2_g=norm2_g, w_in=w_in_r,
        wo_g=w_out[:, :GLA_V].astype(BF16), wo_r=w_out[:, GLA_V:GLA_V + RWKV_W].astype(BF16),
        wo_s=w_out[:, GLA_V + RWKV_W:].astype(BF16),
        gla_wa2=_pad_rows(gla_w_a2, LANES, 0), gla_ba=gla_b_a[:, None, :],
        gla_ng=tile_heads(gla_norm_g, GLA_HEADS)[:, None, :],
        rwkv=dict(
            mu=rwkv_mu[:, None, :], w0=rwkv_w0[:, None, :], ww2=_pad_rows(rwkv_w_w2, LANES, 0),
            a0=rwkv_a0[:, None, :], wa2=_pad_rows(rwkv_w_a2, LANES, RWKV_W_LORA), wg2=rwkv_w_g2,
            kk=rwkv_k_k[:, None, :], ka=rwkv_k_a[:, None, :], rk=rwkv_r_k.reshape(depth, 1, RWKV_W),
            lng=rwkv_ln_g[:, None, :], lnb=rwkv_ln_b[:, None, :]),
        sb_qg=tile_heads(sb_q_g, SB_HEADS)[:, None, :], sb_kg=tile_heads(sb_k_g, SB_HEADS)[:, None, :],
        ffn_w1=ffn_w1.astype(BF16), ffn_w3=ffn_w3.astype(BF16), ffn_w2=ffn_w2.astype(BF16),
        router=jnp.pad(moe_router, ((0, 0), (0, 0), (0, LANES - moe_router.shape[-1]))),
        moe_w1=moe_w1.astype(BF16), moe_w3=moe_w3.astype(BF16), moe_w2=moe_w2.astype(BF16),
    )

    rows = bp + bs
    rows_pad = -(-rows // 8) * 8
    c_all = jnp.pad(jnp.concatenate([c_prompt, c_sample], axis=0), ((0, rows_pad - rows), (0, 0)))
    mod_all = _modulation(c_all, w_ada, b_ada)

    xp = x_prompt.reshape(bp * seq, d)
    xs = x_sample.reshape(bs * dec, d)
    cache_k = jnp.transpose(cache_sb_k, (0, 1, 3, 4, 2))
    cache_v = jnp.transpose(cache_sb_v, (0, 1, 3, 4, 2))
    zero_p = (jnp.zeros((bp, GLA_V, GLA_QK), F32), jnp.zeros((bp, HEAD_DIM, RWKV_W), F32),
              jnp.zeros((bp, 1, RWKV_COLS), F32))
    st_p, st_s = [], []
    for l in range(depth):
        mod_p = mod_all[l, :bp][:, None, :]
        mod_s = jnp.repeat(mod_all[l, bp:rows], dec, axis=0)[None]
        xp, sp = _layer(l, xp, mod_p, bp, zero_p, W,
                        lambda q, kb, vb, k_rows, v_rows: _sb_prompt(q, kb, vb, bp), sb_bias[l])
        state_s = (_gla_state_to_bd(state_gla[l]), _rwkv_state_in(state_rwkv[l]), state_shift[l][:, None, :])
        xs, ss = _layer(l, xs, mod_s, bs, state_s, W,
                        lambda q, kb, vb, k_rows, v_rows: _sb_sample(q, k_rows, v_rows, cache_k, cache_v, l,
                                                                     page_table, sb_bias[l], bs))
        st_p.append(sp)
        st_s.append(ss)

    def stack(sts, idx, shape):
        return jnp.stack([s[idx] for s in sts]).reshape((depth,) + shape)

    outs = [xp.reshape(bp, seq, d), xs.reshape(bs, dec, d)]
    for sts, b, t in ((st_p, bp, seq), (st_s, bs, dec)):
        outs += [
            stack(sts, 0, (b, t, SB_HEADS, HEAD_DIM)), stack(sts, 1, (b, t, SB_HEADS, HEAD_DIM)),
            stack(sts, 2, (b, GLA_HEADS, GLA_DK, GLA_DV)), stack(sts, 3, (b, RWKV_HEADS, HEAD_DIM, HEAD_DIM)),
            stack(sts, 4, (b, RWKV_COLS)),
        ]
    return tuple(outs)
```

```python
import functools
import math

import numpy as np
import jax
import jax.numpy as jnp
from jax import lax
from jax.experimental import pallas as pl
from jax.experimental.pallas import tpu as pltpu

F32 = jnp.float32
BF16 = jnp.bfloat16
HIGHEST = lax.Precision.HIGHEST

HEAD_DIM = 64
GLA_HEADS = 4
GLA_DK = 32
GLA_DV = 64
GLA_LORA = 16
GLA_TAU = 16.0
GLA_CHUNK = 16
RWKV_HEADS = 4
RWKV_W_LORA = 64
RWKV_A_LORA = 64
RWKV_G_LORA = 128
RWKV_LN_EPS = 64e-5
SB_HEADS = 8
TOP_K = 2
EPS = 1e-6
LOG2E = math.log2(math.e)
SB_DEAD_BITS = 160.0

GLA_QK = GLA_HEADS * GLA_DK
GLA_V = GLA_HEADS * GLA_DV
GLA_PCOLS = 2 * GLA_QK + 2 * GLA_V + 128
RWKV_W = RWKV_HEADS * HEAD_DIM
RWKV_COLS = 3 * RWKV_W + RWKV_W_LORA + RWKV_A_LORA + RWKV_G_LORA
SB_W = SB_HEADS * HEAD_DIM
SB_COLS = 3 * SB_W
LANES = 128
PAGE = 128
VMEM_LIMIT = 56 * 1024 * 1024


def _cparams(sem):
    return pltpu.CompilerParams(dimension_semantics=sem, vmem_limit_bytes=VMEM_LIMIT)


def _tile(n, pref):
    if n <= pref:
        return n
    t = pref
    while n % t:
        t -= 8
    return t


def _block_ones(n, group, value=1.0):
    idx = np.arange(n) // group
    return jnp.asarray((idx[:, None] == idx[None, :]).astype(np.float32) * value)


def _dot(a, b):
    return jnp.dot(a, b, preferred_element_type=F32)


def _dot_hi(a, b):
    return jnp.dot(a, b, preferred_element_type=F32, precision=HIGHEST)


def _dot_nt(a, b, precision=None):
    return lax.dot_general(a, b, (((1,), (1,)), ((), ())), preferred_element_type=F32, precision=precision)


def _dot_tn(a, b, precision=None):
    return lax.dot_general(a, b, (((0,), (0,)), ((), ())), preferred_element_type=F32, precision=precision)


def _dot_split(x, m_bf16):
    hi = x.astype(BF16)
    lo = (x - hi.astype(F32)).astype(BF16)
    return _dot(hi, m_bf16) + _dot(lo, m_bf16)


def _sigmoid(x):
    return 1.0 / (1.0 + jnp.exp(-x))


def _softplus(x):
    return jnp.maximum(x, 0.0) + jnp.log(1.0 + jnp.exp(-jnp.abs(x)))


def _mod_kernel(c_ref, w_ref, b_ref, o_ref):
    c = c_ref[...]
    o_ref[0] = _dot_hi(c * _sigmoid(c), w_ref[0]) + b_ref[0]


def _modulation(c_all, w_ada, b_ada):
    depth, d, d6 = w_ada.shape
    r = c_all.shape[0]
    tn = 1024
    return pl.pallas_call(
        _mod_kernel,
        grid=(depth, d6 // tn),
        in_specs=[
            pl.BlockSpec((r, d), lambda l, j: (0, 0)),
            pl.BlockSpec((1, d, tn), lambda l, j: (l, 0, j)),
            pl.BlockSpec((1, 1, tn), lambda l, j: (l, 0, j)),
        ],
        out_specs=pl.BlockSpec((1, r, tn), lambda l, j: (l, 0, j)),
        out_shape=jax.ShapeDtypeStruct((depth, r, d6), F32),
        compiler_params=_cparams(("arbitrary", "arbitrary")),
    )(c_all, w_ada, b_ada.reshape(depth, 1, d6))


def _mod_spec(mod, tm, tpb, chunk):
    d = mod.shape[2] // 6
    if mod.shape[1] == 1:
        return pl.BlockSpec((1, 1, d), lambda i, *_: (i // tpb, 0, chunk))
    return pl.BlockSpec((1, tm, d), lambda i, *_: (i // tpb, i % tpb, chunk))


def _inproj_kernel(x_ref, sh_ref, sc_ref, g_ref, w_ref, pg_ref, pr_ref, ps_ref):
    x = x_ref[...]
    ms = jnp.mean(x * x, axis=-1, keepdims=True)
    h = x * lax.rsqrt(ms + EPS) * g_ref[...]
    h = h * (1.0 + sc_ref[0]) + sh_ref[0]
    p = _dot(h.astype(BF16), w_ref[...])
    pg_ref[...] = p[:, :GLA_PCOLS]
    pr_ref[...] = p[:, GLA_PCOLS:GLA_PCOLS + RWKV_COLS]
    ps_ref[...] = p[:, GLA_PCOLS + RWKV_COLS:]


def _inproj(x, mod, norm_g, w):
    n, d = x.shape
    nb = mod.shape[0]
    tm = _tile(n // nb, 256)
    tpb = (n // nb) // tm
    ncols = w.shape[1]
    return pl.pallas_call(
        _inproj_kernel,
        grid=(n // tm,),
        in_specs=[
            pl.BlockSpec((tm, d), lambda i: (i, 0)),
            _mod_spec(mod, tm, tpb, 0),
            _mod_spec(mod, tm, tpb, 1),
            pl.BlockSpec((1, d), lambda i: (0, 0)),
            pl.BlockSpec((d, ncols), lambda i: (0, 0)),
        ],
        out_specs=[
            pl.BlockSpec((tm, GLA_PCOLS), lambda i: (i, 0)),
            pl.BlockSpec((tm, RWKV_COLS), lambda i: (i, 0)),
            pl.BlockSpec((tm, SB_COLS), lambda i: (i, 0)),
        ],
        out_shape=[
            jax.ShapeDtypeStruct((n, GLA_PCOLS), F32),
            jax.ShapeDtypeStruct((n, RWKV_COLS), F32),
            jax.ShapeDtypeStruct((n, SB_COLS), F32),
        ],
        compiler_params=_cparams(("arbitrary",)),
    )(x, mod, mod, norm_g.reshape(1, d), w)


def _gla_kernel(p_ref, s0_ref, wa2_ref, ba_ref, ng_ref, hexp_ref, havg_ref, bd_ref, tril_ref, ones_ref,
                o_ref, s_ref, b_scr, qin_scr, kout_scr, dec_scr, o_scr, *, chunk, n_chunks):
    c_len = chunk

    @pl.when(pl.program_id(1) == 0)
    def _():
        s_ref[0] = s0_ref[0]

    glow = p_ref[0, :, 2 * GLA_QK + 2 * GLA_V:]
    x = _dot_hi(glow, wa2_ref[...]) + ba_ref[...]
    la = -_softplus(-x) * (1.0 / GLA_TAU)
    la_hi = la.astype(BF16)
    la_lo = (la - la_hi.astype(F32)).astype(BF16)
    b = _dot(tril_ref[...], la_hi) + _dot(tril_ref[...], la_lo)
    b_tot = _dot(ones_ref[...], la_hi) + _dot(ones_ref[...], la_lo)
    b_scr[...] = b
    qin_scr[...] = p_ref[0, :, 0:GLA_QK] * (GLA_DK ** -0.5) * jnp.exp(b)
    kout_scr[...] = p_ref[0, :, GLA_QK:2 * GLA_QK] * jnp.exp(b_tot - b)
    dec_scr[...] = jnp.exp(b_tot)
    row = lax.broadcasted_iota(jnp.int32, (c_len, 1), 0)
    hexp = hexp_ref[...]
    bd = bd_ref[...]

    def body(c, carry):
        sl = pl.ds(pl.multiple_of(c * c_len, c_len), c_len)
        q = p_ref[0, sl, 0:GLA_QK] * (GLA_DK ** -0.5)
        k = p_ref[0, sl, GLA_QK:2 * GLA_QK]
        v = p_ref[0, sl, 2 * GLA_QK:2 * GLA_QK + GLA_V]
        b = b_scr[sl, :]
        zs = []
        for s in range(c_len):
            m = row >= s
            e = jnp.exp(jnp.where(m, b - b[s:s + 1, :], 0.0))
            zs.append(jnp.where(m, q * k[s:s + 1, :] * e, 0.0))
        a_exp = _dot_split(jnp.concatenate(zs, axis=0), hexp)
        o = jnp.zeros((c_len, GLA_V), F32)
        for s in range(c_len):
            o = o + a_exp[s * c_len:(s + 1) * c_len, :] * v[s:s + 1, :]
        st = s_ref[0]
        o_scr[sl, :] = o + _dot_nt(qin_scr[sl, :], st, HIGHEST)
        s_ref[0] = st * dec_scr[pl.ds(c * c_len, 1), :] + _dot_tn(v, kout_scr[sl, :], HIGHEST) * bd
        return carry

    lax.fori_loop(0, n_chunks, body, 0, unroll=min(4, n_chunks))
    o = o_scr[...]
    r = p_ref[0, :, 2 * GLA_QK + GLA_V:2 * GLA_QK + 2 * GLA_V]
    on = o * lax.rsqrt(_dot_split(o * o, havg_ref[...]) + EPS) * ng_ref[...]
    o_ref[0] = (on * (r * _sigmoid(r))).astype(o_ref.dtype)


def _gla(p_gla, s0t, wa2p, ba, ng_t, batch, chunk):
    n = p_gla.shape[0]
    t = n // batch
    tt = _tile(t, 512)
    n_chunks = tt // chunk
    p3 = p_gla.reshape(batch, t, GLA_PCOLS)
    hexp = jnp.asarray((np.arange(GLA_QK)[:, None] // GLA_DK == np.arange(GLA_V)[None, :] // GLA_DV).astype(np.float32))
    havg = _block_ones(GLA_V, GLA_DV, 1.0 / GLA_DV)
    bd = jnp.asarray((np.arange(GLA_V)[:, None] // GLA_DV == np.arange(GLA_QK)[None, :] // GLA_DK).astype(np.float32))
    rows = np.arange(tt)
    same = rows[:, None] // chunk == rows[None, :] // chunk
    tril = jnp.asarray((same & (rows[:, None] >= rows[None, :])).astype(np.float32)).astype(BF16)
    ones = jnp.asarray(same.astype(np.float32)).astype(BF16)
    const = lambda shape: pl.BlockSpec(shape, lambda b, i: (0,) * len(shape))
    scr = lambda w: pltpu.VMEM((tt, w), F32)
    o, s = pl.pallas_call(
        functools.partial(_gla_kernel, chunk=chunk, n_chunks=n_chunks),
        grid=(batch, t // tt),
        in_specs=[
            pl.BlockSpec((1, tt, GLA_PCOLS), lambda b, i: (b, i, 0)),
            pl.BlockSpec((1, GLA_V, GLA_QK), lambda b, i: (b, 0, 0)),
            const((LANES, GLA_QK)), const((1, GLA_QK)), const((1, GLA_V)),
            const((GLA_QK, GLA_V)), const((GLA_V, GLA_V)), const((GLA_V, GLA_QK)), const((tt, tt)), const((tt, tt)),
        ],
        out_specs=[
            pl.BlockSpec((1, tt, GLA_V), lambda b, i: (b, i, 0)),
            pl.BlockSpec((1, GLA_V, GLA_QK), lambda b, i: (b, 0, 0)),
        ],
        out_shape=[
            jax.ShapeDtypeStruct((batch, t, GLA_V), BF16),
            jax.ShapeDtypeStruct((batch, GLA_V, GLA_QK), F32),
        ],
        scratch_shapes=[scr(GLA_QK), scr(GLA_QK), scr(GLA_QK), scr(GLA_QK), scr(GLA_V)],
        compiler_params=_cparams(("arbitrary", "arbitrary")),
    )(p3, s0t, wa2p, ba, ng_t, hexp.astype(BF16), havg.astype(BF16), bd, tril, ones)
    return o.reshape(n, GLA_V), s


def _gla_state_to_bd(s0):
    b = s0.shape[0]
    eye = jnp.eye(GLA_HEADS, dtype=s0.dtype)
    st = jnp.einsum('bhkv,hg->bhvgk', s0, eye)
    return st.reshape(b, GLA_V, GLA_QK)


def _gla_state_from_bd(st):
    b = st.shape[0]
    s5 = st.reshape(b, GLA_HEADS, GLA_DV, GLA_HEADS, GLA_DK)
    diag = jnp.stack([s5[:, h, :, h, :] for h in range(GLA_HEADS)], axis=1)
    return jnp.swapaxes(diag, 2, 3)


def _rwkv_kernel(p_ref, sh0_ref, s0_ref, mu_ref, w0_ref, ww2_ref, a0_ref, wa2_ref, wg2_ref, kk_ref, ka_ref,
                 rk_ref, lng_ref, lnb_ref, ones_ref, onesb_ref, avg_ref, sel_ref,
                 o_ref, s_ref, prev_scr, nkk_scr, d_scr, b_scr, k_scr, r_scr, v_scr, y_scr, *, nbb, tt):
    first = pl.program_id(1) == 0

    @pl.when(first)
    def _():
        s_ref[...] = s0_ref[...]
        prev_scr[...] = sh0_ref[...]

    ones_f = ones_ref[...]
    ones_b = onesb_ref[...]
    avg = avg_ref[...]
    sel = sel_ref[...]
    row = lax.broadcasted_iota(jnp.int32, (tt, 1), 0)
    gate = []
    for bb in range(nbb):
        p = p_ref[bb]
        p_prev = jnp.where(row == 0, prev_scr[bb], pltpu.roll(p, 1, axis=0))
        prev_scr[bb] = p[tt - 1:tt, :]
        xm = p + (p_prev - p) * mu_ref[...]
        r = xm[:, 0:RWKV_W]
        k = xm[:, RWKV_W:2 * RWKV_W]
        v = xm[:, 2 * RWKV_W:3 * RWKV_W]
        xwa = xm[:, 3 * RWKV_W:3 * RWKV_W + LANES]
        xg = xm[:, 3 * RWKV_W + LANES:]
        w = -_softplus(-(w0_ref[...] + _dot_hi(jnp.tanh(xwa), ww2_ref[...]))) - 0.5
        a = _sigmoid(a0_ref[...] + _dot_hi(xwa, wa2_ref[...]))
        gate.append(_dot_hi(_sigmoid(xg), wg2_ref[...]))
        kk = k * kk_ref[...]
        nrm = jnp.sqrt(_dot_hi(kk * kk, ones_f))
        kk = kk / jnp.maximum(nrm, 1e-12)
        k2 = k * (1.0 + (a - 1.0) * ka_ref[...])
        nkk_scr[bb] = -kk
        d_scr[bb] = jnp.exp(-jnp.exp(w))
        b_scr[bb] = kk * a
        k_scr[bb] = k2
        r_scr[bb] = r
        v_scr[bb] = v

    def head_sums(xs, split):
        stacked = jnp.concatenate([x[:, half * LANES:(half + 1) * LANES] for x in xs for half in range(2)], axis=0)
        res = _dot_split(stacked, ones_b) if split else _dot(stacked.astype(BF16), ones_b)
        out, row = [], 0
        for x in xs:
            n_rows = x.shape[0]
            out.append(jnp.concatenate([res[row:row + n_rows], res[row + n_rows:row + 2 * n_rows]], axis=1))
            row += 2 * n_rows
        return out

    def read_out(s, bb, ts):
        yb, = head_sums([s * jnp.broadcast_to(r_scr[bb, ts, :], (HEAD_DIM, RWKV_W))], False)
        y_scr[bb, ts, :] = jnp.sum(yb * sel, axis=0, keepdims=True)

    def step(t, states):
        ts = pl.ds(t, 1)
        tp = pl.ds(jnp.maximum(t - 1, 0), 1)
        bc = lambda ref, bb, rows: jnp.broadcast_to(ref[bb, rows, :], (HEAD_DIM, RWKV_W))
        sas = head_sums([states[bb] * bc(nkk_scr, bb, ts) for bb in range(nbb)], True)
        side = head_sums([jnp.concatenate([states[bb] * bc(r_scr, bb, tp), sel * bc(v_scr, bb, ts)], axis=0)
                          for bb in range(nbb)], False)
        new = []
        for bb in range(nbb):
            y_scr[bb, tp, :] = jnp.sum(side[bb][0:HEAD_DIM] * sel, axis=0, keepdims=True)
            vb = side[bb][HEAD_DIM:]
            new.append(states[bb] * bc(d_scr, bb, ts) + sas[bb] * bc(b_scr, bb, ts) + vb * bc(k_scr, bb, ts))
        return tuple(new)

    states = lax.fori_loop(0, tt, step, tuple(s_ref[bb] for bb in range(nbb)))
    for bb in range(nbb):
        s_ref[bb] = states[bb]
        read_out(states[bb], bb, pl.ds(tt - 1, 1))
        y = y_scr[bb]
        mean = _dot_hi(y, avg)
        yc = y - mean
        var = _dot_hi(yc * yc, avg)
        yn = yc * lax.rsqrt(var + RWKV_LN_EPS) * lng_ref[...] + lnb_ref[...]
        bonus = _dot_hi(r_scr[bb] * k_scr[bb] * rk_ref[...], ones_f) * v_scr[bb]
        o_ref[bb] = ((yn + bonus) * gate[bb]).astype(o_ref.dtype)


def _rwkv(p_rwkv, shift0, s0, wl, batch, nbb):
    n = p_rwkv.shape[0]
    t = n // batch
    tt = _tile(t, 256)
    p3 = p_rwkv.reshape(batch, t, RWKV_COLS)
    ones_f = _block_ones(RWKV_W, HEAD_DIM)
    avg = _block_ones(RWKV_W, HEAD_DIM, 1.0 / HEAD_DIM)
    sel = jnp.asarray((np.arange(HEAD_DIM)[:, None] == np.arange(RWKV_W)[None, :] % HEAD_DIM).astype(np.float32))
    const = lambda shape: pl.BlockSpec(shape, lambda b, i: (0,) * len(shape))
    row = const((1, RWKV_W))
    scr = lambda: pltpu.VMEM((nbb, tt, RWKV_W), F32)
    o, s = pl.pallas_call(
        functools.partial(_rwkv_kernel, nbb=nbb, tt=tt),
        grid=(batch // nbb, t // tt),
        in_specs=[
            pl.BlockSpec((nbb, tt, RWKV_COLS), lambda b, i: (b, i, 0)),
            pl.BlockSpec((nbb, 1, RWKV_COLS), lambda b, i: (b, 0, 0)),
            pl.BlockSpec((nbb, HEAD_DIM, RWKV_W), lambda b, i: (b, 0, 0)),
            const((1, RWKV_COLS)), row, const((LANES, RWKV_W)), row, const((LANES, RWKV_W)),
            const((RWKV_G_LORA, RWKV_W)), row, row, row, row, row,
            const((RWKV_W, RWKV_W)), const((LANES, LANES)), const((RWKV_W, RWKV_W)), const((HEAD_DIM, RWKV_W)),
        ],
        out_specs=[
            pl.BlockSpec((nbb, tt, RWKV_W), lambda b, i: (b, i, 0)),
            pl.BlockSpec((nbb, HEAD_DIM, RWKV_W), lambda b, i: (b, 0, 0)),
        ],
        out_shape=[
            jax.ShapeDtypeStruct((batch, t, RWKV_W), BF16),
            jax.ShapeDtypeStruct((batch, HEAD_DIM, RWKV_W), F32),
        ],
        scratch_shapes=[pltpu.VMEM((nbb, 1, RWKV_COLS), F32)] + [scr() for _ in range(7)],
        compiler_params=_cparams(("arbitrary", "arbitrary")),
    )(p3, shift0, s0, wl['mu'], wl['w0'], wl['ww2'], wl['a0'], wl['wa2'], wl['wg2'], wl['kk'], wl['ka'],
      wl['rk'], wl['lng'], wl['lnb'], ones_f, _block_ones(LANES, HEAD_DIM).astype(BF16), avg, sel)
    return o.reshape(n, RWKV_W), s


def _rwkv_state_in(s0):
    b = s0.shape[0]
    return jnp.transpose(s0, (0, 2, 1, 3)).reshape(b, HEAD_DIM, RWKV_W)


def _rwkv_state_out(s):
    b = s.shape[0]
    return jnp.transpose(s.reshape(b, HEAD_DIM, RWKV_HEADS, HEAD_DIM), (0, 2, 1, 3))


def _sbprep_kernel(p_ref, qg_ref, kg_ref, avg_ref, *rest, aug):
    if aug:
        qrow_ref, krow_ref, q_ref, k_ref, kb_ref, vb_ref = rest
    else:
        q_ref, k_ref, kb_ref, vb_ref = rest
    avg = avg_ref[...]
    q = p_ref[:, 0:SB_W]
    k = p_ref[:, SB_W:2 * SB_W]
    qn = q * lax.rsqrt(_dot_hi(q * q, avg) + EPS) * (qg_ref[...] * (HEAD_DIM ** -0.5 * LOG2E))
    kn = k * lax.rsqrt(_dot_hi(k * k, avg) + EPS) * kg_ref[...]
    k_ref[...] = kn
    vb_ref[...] = p_ref[:, 2 * SB_W:].astype(BF16)
    if aug:
        lo = lax.broadcasted_iota(jnp.int32, (1, LANES), 1) < HEAD_DIM

        def widen(x, row):
            parts = []
            for h in range(SB_HEADS):
                blk = x[:, (h // 2) * LANES:(h // 2 + 1) * LANES]
                if h % 2:
                    blk = pltpu.roll(blk, HEAD_DIM, axis=1)
                parts.append(jnp.where(lo, blk, 0.0))
            return (jnp.concatenate(parts, axis=1) + row).astype(BF16)

        q_ref[...] = widen(qn, qrow_ref[...])
        kb_ref[...] = widen(kn, krow_ref[...])
    else:
        q_ref[...] = qn.astype(BF16)
        kb_ref[...] = kn.astype(BF16)


def _sbprep(p_sb, qg_t, kg_t, bias=None):
    n = p_sb.shape[0]
    tm = _tile(n, 512)
    aug = bias is not None
    avg = _block_ones(SB_W, HEAD_DIM, 1.0 / HEAD_DIM)
    wide = SB_HEADS * LANES if aug else SB_W
    out = lambda w: pl.BlockSpec((tm, w), lambda i: (i, 0))
    const = lambda shape: pl.BlockSpec(shape, lambda i: (0,) * len(shape))
    in_specs = [pl.BlockSpec((tm, SB_COLS), lambda i: (i, 0)), const((1, SB_W)), const((1, SB_W)), const((SB_W, SB_W))]
    args = [p_sb, qg_t, kg_t, avg]
    if aug:
        b2 = bias * LOG2E
        hi = b2.astype(BF16).astype(F32)
        lo = (b2 - hi).astype(BF16).astype(F32)
        qrow = jnp.zeros((SB_HEADS, LANES), F32).at[:, HEAD_DIM].set(hi).at[:, HEAD_DIM + 1].set(lo)
        krow = jnp.zeros((SB_HEADS, LANES), F32).at[:, HEAD_DIM:HEAD_DIM + 2].set(1.0)
        in_specs += [const((1, wide)), const((1, wide))]
        args += [qrow.reshape(1, wide), krow.reshape(1, wide)]
    return pl.pallas_call(
        functools.partial(_sbprep_kernel, aug=aug),
        grid=(n // tm,),
        in_specs=in_specs,
        out_specs=[out(wide), out(SB_W), out(wide), out(SB_W)],
        out_shape=[
            jax.ShapeDtypeStruct((n, wide), BF16),
            jax.ShapeDtypeStruct((n, SB_W), F32),
            jax.ShapeDtypeStruct((n, wide), BF16),
            jax.ShapeDtypeStruct((n, SB_W), BF16),
        ],
        compiler_params=_cparams(("arbitrary",)),
    )(*args)


def _sb_keep(z, mask):
    keep = jnp.maximum(z, 0.0) + jnp.log2(1.0 + jnp.exp2(-jnp.abs(z)))
    if mask is not None:
        keep = jnp.where(mask, keep, 0.0)
    return z - keep, keep.astype(BF16)


def _sb_logits_stage(z, mask, cum):
    zk, kb = _sb_keep(z, mask)
    tk = z.shape[1]
    cs = _dot(kb, cum)
    return zk - cs[:, :tk], cs[:, tk:]


def _sb_weights(part, r_run, mask):
    reps = part.shape[1] // r_run.shape[1]
    e = jnp.exp2(part - (jnp.concatenate([r_run] * reps, axis=1) if reps > 1 else r_run))
    if mask is not None:
        e = jnp.where(mask, e, 0.0)
    return e


def _sb_prompt_kernel(q_ref, k_ref, v_ref, later_ref, o_ref, zk_scr, kb_scr, tot_scr, *, tq):
    i = pl.program_id(2)
    later = later_ref[...]
    rows = lax.broadcasted_iota(jnp.int32, (tq, tq), 0)
    cols = lax.broadcasted_iota(jnp.int32, (tq, tq), 1)
    diag = cols < rows

    def stage1(j, mask):
        ks = pl.ds(pl.multiple_of(j * tq, tq), tq)
        out = []
        for u in range(2):
            z = _dot_nt(q_ref[:, u * LANES:(u + 1) * LANES], k_ref[ks, u * LANES:(u + 1) * LANES])
            keep = jnp.maximum(z, 0.0) + jnp.log2(1.0 + jnp.exp2(-jnp.abs(z)))
            if mask is not None:
                keep = jnp.where(mask, keep, 0.0)
            tot = jnp.broadcast_to(jnp.sum(keep, axis=1, keepdims=True), (tq, LANES))
            out.append((z - keep, keep.astype(BF16), tot))
        return out

    def stage2(j, pend, carry, mask, valid):
        vb = v_ref[pl.ds(pl.multiple_of(j * tq, tq), tq), :]
        out = []
        for u in range(2):
            zk, kb, tot = pend[u]
            acc, r_run = carry[u]
            pv = _dot(_sb_weights(zk - _dot(kb, later), r_run, mask).astype(BF16), vb)
            if valid is not None:
                pv = jnp.where(valid, pv, 0.0)
                tot = jnp.where(valid, tot, 0.0)
            out.append((acc + pv, r_run + tot))
        return tuple(out)

    carry = tuple((jnp.zeros((tq, LANES), F32), jnp.zeros((tq, LANES), F32)) for _ in range(2))
    carry = stage2(i, stage1(i, diag), carry, diag, None)
    zk_scr[...] = jnp.zeros_like(zk_scr)
    kb_scr[...] = jnp.zeros_like(kb_scr)
    tot_scr[...] = jnp.zeros_like(tot_scr)

    def half_trip(n, slot, carry):
        nxt = stage1(jnp.maximum(i - 1 - n, 0), None)
        pend = [(zk_scr[slot, u], kb_scr[slot, u], tot_scr[slot, u]) for u in range(2)]
        carry = stage2(jnp.maximum(i - n, 0), pend, carry, None, (n >= 1) & (n <= i))
        for u in range(2):
            zk_scr[1 - slot, u] = nxt[u][0]
            kb_scr[1 - slot, u] = nxt[u][1]
            tot_scr[1 - slot, u] = nxt[u][2]
        return carry

    def body(state):
        m, _, carry = state
        live = jnp.minimum(jnp.min(carry[0][1]), jnp.min(carry[1][1])) < SB_DEAD_BITS
        carry = half_trip(2 * m, 0, carry)
        carry = half_trip(2 * m + 1, 1, carry)
        return m + 1, live, carry

    _, _, carry = lax.while_loop(lambda s: (2 * s[0] < i + 1) & s[1], body, (jnp.int32(0), jnp.bool_(True), carry))
    lo = lax.broadcasted_iota(jnp.int32, (1, LANES), 1) < HEAD_DIM
    o_ref[...] = jnp.where(lo, carry[0][0], carry[1][0]).astype(o_ref.dtype)


def _cum_matrix(tk):
    j = np.arange(tk)
    later = (j[:, None] > j[None, :]).astype(np.float32)
    return jnp.asarray(np.concatenate([later, np.ones((tk, LANES), np.float32)], axis=1)).astype(BF16)


def _sb_prompt(q_aug, k_aug, vb, batch):
    n = vb.shape[0]
    t = n // batch
    tq = _tile(t, 256)
    tpb = t // tq
    return pl.pallas_call(
        functools.partial(_sb_prompt_kernel, tq=tq),
        grid=(batch, SB_HEADS // 2, tpb),
        in_specs=[
            pl.BlockSpec((tq, 2 * LANES), lambda b, h, i: (b * tpb + i, h)),
            pl.BlockSpec((t, 2 * LANES), lambda b, h, i: (b, h)),
            pl.BlockSpec((t, LANES), lambda b, h, i: (b, h)),
            pl.BlockSpec((tq, tq), lambda b, h, i: (0, 0)),
        ],
        out_specs=pl.BlockSpec((tq, LANES), lambda b, h, i: (b * tpb + i, h)),
        out_shape=jax.ShapeDtypeStruct((n, SB_W), BF16),
        scratch_shapes=[pltpu.VMEM((2, 2, tq, tq), F32), pltpu.VMEM((2, 2, tq, tq), BF16),
                        pltpu.VMEM((2, 2, tq, LANES), F32)],
        compiler_params=_cparams(("arbitrary", "arbitrary", "arbitrary")),
    )(q_aug, k_aug, vb, _cum_matrix(tq)[:, :tq])


def _sb_sample_kernel(pt_ref, q_ref, kn_ref, vn_ref, bias_ref, bd_ref, cum_ref, *rest, n_pages, ppb, dec):
    k_refs = rest[:ppb]
    v_refs = rest[ppb:2 * ppb]
    o_ref, qbd_scr, acc_scr, run_scr = rest[2 * ppb:]
    g = pl.program_id(1)
    rows_n = SB_HEADS * dec
    bd = bd_ref[...]
    cum = cum_ref[...]
    bias = bias_ref[...]

    def logits(kt, mask):
        return _sb_logits_stage(_dot(qbd_scr[...], kt.astype(BF16)) + bias, mask, cum)

    def attend(kt, vt, mask):
        part, tot = logits(kt, mask)
        acc_scr[...] += _dot_nt(_sb_weights(part, run_scr[...], mask).astype(BF16), vt.astype(BF16))
        run_scr[...] += tot

    @pl.when(g == 0)
    def _():
        q = q_ref[0].astype(F32)
        qbd_scr[...] = (jnp.concatenate([q] * SB_HEADS, axis=0) * bd).astype(BF16)
        acc_scr[...] = jnp.zeros_like(acc_scr)
        run_scr[...] = jnp.zeros_like(run_scr)
        qi = lax.broadcasted_iota(jnp.int32, (rows_n, PAGE), 0) % dec
        kj = lax.broadcasted_iota(jnp.int32, (rows_n, PAGE), 1)
        attend(kn_ref[0], vn_ref[0], kj < qi)

    pend = [logits(k_refs[pidx][0, 0].reshape(SB_W, PAGE), None) for pidx in range(ppb)]
    r_run = run_scr[...]
    acc = acc_scr[...]
    for pidx in range(ppb):
        part, tot = pend[pidx]
        vt = v_refs[pidx][0, 0].reshape(SB_W, PAGE).astype(BF16)
        acc = acc + _dot_nt(_sb_weights(part, r_run, None).astype(BF16), vt)
        r_run = r_run + tot
    run_scr[...] = r_run
    acc_scr[...] = acc

    @pl.when(g == n_pages // ppb - 1)
    def _():
        masked = acc_scr[...] * bd
        o = masked[0:dec, :]
        for h in range(1, SB_HEADS):
            o = o + masked[h * dec:(h + 1) * dec, :]
        o_ref[0] = o.astype(o_ref.dtype)


def _sb_sample(qs, k_rows, v_rows, cache_k, cache_v, layer, page_table, bias, batch):
    n = qs.shape[0]
    dec = n // batch
    n_pages = page_table.shape[1]
    ppb = 8 if n_pages % 8 == 0 else 1
    rows_n = SB_HEADS * dec
    q3 = qs.reshape(batch, dec, SB_W)

    def as_page(a):
        a = jnp.swapaxes(a.reshape(batch, dec, SB_W), 1, 2)
        return jnp.pad(a, ((0, 0), (0, 0), (0, PAGE - dec)))

    bias_rows = jnp.broadcast_to(jnp.repeat(bias * LOG2E, dec)[:, None], (rows_n, PAGE))
    bd = jnp.asarray((np.arange(rows_n)[:, None] // dec == np.arange(SB_W)[None, :] // HEAD_DIM).astype(np.float32))

    def page_spec(pidx):
        return pl.BlockSpec((1, 1, SB_HEADS, HEAD_DIM, PAGE),
                            lambda b, g, pt: (layer, pt[b, n_pages - 1 - (g * ppb + pidx)], 0, 0, 0))

    const = lambda shape: pl.BlockSpec(shape, lambda b, g, pt: (0,) * len(shape))
    grid_spec = pltpu.PrefetchScalarGridSpec(
        num_scalar_prefetch=1,
        grid=(batch, n_pages // ppb),
        in_specs=[
            pl.BlockSpec((1, dec, SB_W), lambda b, g, pt: (b, 0, 0)),
            pl.BlockSpec((1, SB_W, PAGE), lambda b, g, pt: (b, 0, 0)),
            pl.BlockSpec((1, SB_W, PAGE), lambda b, g, pt: (b, 0, 0)),
            const((rows_n, PAGE)), const((rows_n, SB_W)), const((PAGE, PAGE + LANES)),
        ] + [page_spec(pidx) for pidx in range(ppb)] * 2,
        out_specs=pl.BlockSpec((1, dec, SB_W), lambda b, g, pt: (b, 0, 0)),
        scratch_shapes=[
            pltpu.VMEM((rows_n, SB_W), BF16),
            pltpu.VMEM((rows_n, SB_W), F32),
            pltpu.VMEM((rows_n, PAGE), F32),
        ],
    )
    o = pl.pallas_call(
        functools.partial(_sb_sample_kernel, n_pages=n_pages, ppb=ppb, dec=dec),
        grid_spec=grid_spec,
        out_shape=jax.ShapeDtypeStruct((batch, dec, SB_W), BF16),
        compiler_params=_cparams(("arbitrary", "arbitrary")),
    )(page_table, q3, as_page(k_rows), as_page(v_rows), bias_rows, bd, _cum_matrix(PAGE),
      *([cache_k] * ppb), *([cache_v] * ppb))
    return o.reshape(n, SB_W)


def _outproj_kernel(og_ref, or_ref, os_ref, x_ref, g1_ref, sh_ref, sc_ref, ng_ref, wg_ref, wr_ref, ws_ref,
                    *rest, with_router):
    if with_router:
        router_ref, x1_ref, h2_ref, gates_ref = rest
    else:
        x1_ref, h2_ref = rest
    mix = _dot(og_ref[...], wg_ref[...]) + _dot(or_ref[...], wr_ref[...]) + _dot(os_ref[...], ws_ref[...])
    x1 = x_ref[...] + g1_ref[0] * mix
    x1_ref[...] = x1
    ms = jnp.mean(x1 * x1, axis=-1, keepdims=True)
    h2 = x1 * lax.rsqrt(ms + EPS) * ng_ref[...]
    h2 = h2 * (1.0 + sc_ref[0]) + sh_ref[0]
    h2_ref[...] = h2.astype(BF16)
    if with_router:
        n_exp = 8
        logits = _dot_hi(h2, router_ref[...])
        lane = lax.broadcasted_iota(jnp.int32, logits.shape, 1).astype(F32)
        neg = jnp.float32(-jnp.inf)
        lg = jnp.where(lane < n_exp, logits, neg)
        m1 = jnp.max(lg, axis=1, keepdims=True)
        i1 = jnp.min(jnp.where(lg == m1, lane, float(LANES)), axis=1, keepdims=True)
        lg2 = jnp.where(lane == i1, neg, lg)
        m2 = jnp.max(lg2, axis=1, keepdims=True)
        i2 = jnp.min(jnp.where(lg2 == m2, lane, float(LANES)), axis=1, keepdims=True)
        e2 = jnp.exp(m2 - m1)
        gates_ref[...] = jnp.where(lane == i1, 1.0 / (1.0 + e2), 0.0) + jnp.where(lane == i2, e2 / (1.0 + e2), 0.0)


def _outproj(o_gla, o_rwkv, o_sb, x, mod, norm_g, wg, wr, ws, router):
    n, d = x.shape
    nb = mod.shape[0]
    tm = _tile(n // nb, 512)
    tpb = (n // nb) // tm
    with_router = router is not None
    row = lambda w: pl.BlockSpec((tm, w), lambda i: (i, 0))
    const = lambda shape: pl.BlockSpec(shape, lambda i: (0,) * len(shape))
    in_specs = [row(GLA_V), row(RWKV_W), row(SB_W), row(d),
                _mod_spec(mod, tm, tpb, 2), _mod_spec(mod, tm, tpb, 3), _mod_spec(mod, tm, tpb, 4),
                const((1, d)), const((GLA_V, d)), const((RWKV_W, d)), const((SB_W, d))]
    args = [o_gla, o_rwkv, o_sb, x, mod, mod, mod, norm_g.reshape(1, d), wg, wr, ws]
    out_specs = [row(d), row(d)]
    out_shape = [jax.ShapeDtypeStruct((n, d), F32), jax.ShapeDtypeStruct((n, d), BF16)]
    if with_router:
        in_specs.append(const((d, LANES)))
        args.append(router)
        out_specs.append(row(LANES))
        out_shape.append(jax.ShapeDtypeStruct((n, LANES), F32))
    return pl.pallas_call(
        functools.partial(_outproj_kernel, with_router=with_router),
        grid=(n // tm,),
        in_specs=in_specs, out_specs=out_specs, out_shape=out_shape,
        compiler_params=_cparams(("arbitrary",)),
    )(*args)


def _ffn_kernel(h_ref, x1_ref, g2_ref, *rest, gated):
    if gated:
        gates_ref, w1_ref, w3_ref, w2_ref, o_ref, acc_scr = rest
    else:
        w1_ref, w3_ref, w2_ref, o_ref, acc_scr = rest
    e = pl.program_id(1)
    f = pl.program_id(2)

    @pl.when((e == 0) & (f == 0))
    def _():
        acc_scr[...] = jnp.zeros_like(acc_scr)

    h = h_ref[...]
    a = _dot(h, w1_ref[0])
    t = a * _sigmoid(a) * _dot(h, w3_ref[0])
    if gated:
        gates = gates_ref[...]
        lane = lax.broadcasted_iota(jnp.int32, gates.shape, 1)
        t = t * jnp.sum(jnp.where(lane == e, gates, 0.0), axis=1, keepdims=True)
    acc_scr[...] += _dot(t.astype(BF16), w2_ref[0])

    @pl.when((e == pl.num_programs(1) - 1) & (f == pl.num_programs(2) - 1))
    def _():
        o_ref[...] = x1_ref[...] + g2_ref[0] * acc_scr[...]


def _ffn(h2, x1, mod, w1, w3, w2, gates, tf):
    n, d = x1.shape
    n_exp, _, ff = w1.shape
    nb = mod.shape[0]
    tm = _tile(n // nb, 512)
    tpb = (n // nb) // tm
    gated = gates is not None
    row = lambda w: pl.BlockSpec((tm, w), lambda i, e, f: (i, 0))
    in_specs = [row(d), row(d), _mod_spec(mod, tm, tpb, 5)]
    args = [h2, x1, mod]
    if gated:
        in_specs.append(row(LANES))
        args.append(gates)
    in_specs += [
        pl.BlockSpec((1, d, tf), lambda i, e, f: (e, 0, f)),
        pl.BlockSpec((1, d, tf), lambda i, e, f: (e, 0, f)),
        pl.BlockSpec((1, tf, d), lambda i, e, f: (e, f, 0)),
    ]
    args += [w1, w3, w2]
    return pl.pallas_call(
        functools.partial(_ffn_kernel, gated=gated),
        grid=(n // tm, n_exp, ff // tf),
        in_specs=in_specs,
        out_specs=row(d),
        out_shape=jax.ShapeDtypeStruct((n, d), F32),
        scratch_shapes=[pltpu.VMEM((tm, d), F32)],
        compiler_params=_cparams(("arbitrary", "arbitrary", "arbitrary")),
    )(*args)


def _moe_kernel(cnt_ref, h_ref, x1_ref, g2_ref, gates_ref, pos_ref, w1_ref, w3_ref, w2_ref, o_ref,
                hc_scr, accc_scr, *, tm, ch):
    i = pl.program_id(0)
    e = pl.program_id(1)
    f = pl.program_id(2)
    last_f = f == pl.num_programs(2) - 1
    count = cnt_ref[i, e]
    rank = lax.broadcasted_iota(jnp.int32, (ch, 1), 0)

    def one_hot(c):
        return jnp.where(pos_ref[0, pl.ds(e, 1), :] == rank + c * ch, 1.0, 0.0).astype(BF16)

    @pl.when((e == 0) & (f == 0))
    def _():
        o_ref[...] = jnp.zeros_like(o_ref)

    for c in range(-(-tm // ch)):
        rows = pl.ds(c * ch, ch)

        @pl.when(c * ch < count)
        def _():
            @pl.when(f == 0)
            def _():
                hc_scr[rows, :] = _dot(one_hot(c), h_ref[...]).astype(BF16)
                accc_scr[rows, :] = jnp.zeros((ch, accc_scr.shape[1]), F32)

            hc = hc_scr[rows, :]
            a = _dot(hc, w1_ref[0])
            t = a * _sigmoid(a) * _dot(hc, w3_ref[0])
            accc_scr[rows, :] += _dot(t.astype(BF16), w2_ref[0])

            @pl.when(last_f)
            def _():
                y = accc_scr[rows, :]
                hi = y.astype(BF16)
                lo = (y - hi.astype(F32)).astype(BF16)
                p = one_hot(c)
                gates = gates_ref[...]
                lane = lax.broadcasted_iota(jnp.int32, gates.shape, 1)
                ge = jnp.sum(jnp.where(lane == e, gates, 0.0), axis=1, keepdims=True)
                o_ref[...] += ge * (_dot_tn(p, hi) + _dot_tn(p, lo))

    @pl.when((e == pl.num_programs(1) - 1) & last_f)
    def _():
        o_ref[...] = x1_ref[...] + g2_ref[0] * o_ref[...]


def _moe(h2, x1, mod, w1, w3, w2, gates, tf):
    n, d = x1.shape
    n_exp, _, ff = w1.shape
    nb = mod.shape[0]
    tm = _tile(n // nb, 1024)
    tpb = (n // nb) // tm
    ch = -(-int(tm * TOP_K / n_exp * 1.125) // 32) * 32 if tm >= 512 else tm // 2
    slots = -(-tm // ch)
    n_tiles = n // tm
    routed = (gates[:, :n_exp] > 0.0).reshape(n_tiles, tm, n_exp)
    counts = jnp.sum(routed, axis=1).astype(jnp.int32)
    ranks = jnp.cumsum(routed.astype(jnp.int32), axis=1) - 1
    pos = jnp.swapaxes(jnp.where(routed, ranks, -1), 1, 2)
    row = lambda w: pl.BlockSpec((tm, w), lambda i, e, f, cnt: (i, 0))
    grid_spec = pltpu.PrefetchScalarGridSpec(
        num_scalar_prefetch=1,
        grid=(n_tiles, n_exp, ff // tf),
        in_specs=[
            row(d), row(d), _mod_spec(mod, tm, tpb, 5), row(LANES),
            pl.BlockSpec((1, n_exp, tm), lambda i, e, f, cnt: (i, 0, 0)),
            pl.BlockSpec((1, d, tf), lambda i, e, f, cnt: (e, 0, f)),
            pl.BlockSpec((1, d, tf), lambda i, e, f, cnt: (e, 0, f)),
            pl.BlockSpec((1, tf, d), lambda i, e, f, cnt: (e, f, 0)),
        ],
        out_specs=row(d),
        scratch_shapes=[pltpu.VMEM((slots * ch, d), BF16), pltpu.VMEM((slots * ch, d), F32)],
    )
    return pl.pallas_call(
        functools.partial(_moe_kernel, tm=tm, ch=ch),
        grid_spec=grid_spec,
        out_shape=jax.ShapeDtypeStruct((n, d), F32),
        compiler_params=_cparams(("arbitrary", "arbitrary", "arbitrary")),
    )(counts, h2, x1, mod, gates, pos, w1, w3, w2)


def _pad_rows(w, rows, offset):
    return jnp.pad(w, ((0, 0), (offset, rows - offset - w.shape[1]), (0, 0)))


def _layer(l, x, mod, batch, state, W, sb_fn, sb_bias=None):
    n = x.shape[0]
    gla0, rwkv0, shift0 = state
    p_gla, p_rwkv, p_sb = _inproj(x, mod, W['norm1_g'][l], W['w_in'][l])
    chunk = math.gcd(n // batch, GLA_CHUNK)
    o_gla, gla_st = _gla(p_gla, gla0, W['gla_wa2'][l], W['gla_ba'][l], W['gla_ng'][l], batch, chunk)
    o_rwkv, rwkv_st = _rwkv(p_rwkv, shift0, rwkv0, {k: v[l] for k, v in W['rwkv'].items()}, batch, 2)
    qs, k_rows, kb, vb = _sbprep(p_sb, W['sb_qg'][l], W['sb_kg'][l], sb_bias)
    v_rows = p_sb[:, 2 * SB_W:]
    o_sb = sb_fn(qs, kb, vb, k_rows, v_rows)
    router = W['router'][l // 2] if l % 2 else None
    outs = _outproj(o_gla, o_rwkv, o_sb, x, mod, W['norm2_g'][l], W['wo_g'][l], W['wo_r'][l], W['wo_s'][l], router)
    if l % 2 == 0:
        x1, h2 = outs
        x = _ffn(h2, x1, mod, W['ffn_w1'][l // 2][None], W['ffn_w3'][l // 2][None], W['ffn_w2'][l // 2][None],
                 None, W['ffn_w1'].shape[-1] // 2)
    else:
        x1, h2, gates = outs
        x = _moe(h2, x1, mod, W['moe_w1'][l // 2], W['moe_w3'][l // 2], W['moe_w2'][l // 2], gates,
                 W['moe_w1'].shape[-1] // 7)
    t = n // batch
    shift_row = p_rwkv.reshape(batch, t, RWKV_COLS)[:, -1]
    return x, (k_rows, v_rows, _gla_state_from_bd(gla_st), _rwkv_state_out(rwkv_st), shift_row)


def kernel(x_prompt, x_sample, cache_sb_k, cache_sb_v, state_gla, state_rwkv, state_shift, page_table, c_prompt, c_sample, norm1_g, norm2_g, w_ada, b_ada, w_in, w_out, gla_w_a2, gla_b_a, gla_norm_g, rwkv_mu, rwkv_w0, rwkv_w_w2, rwkv_a0, rwkv_w_a2, rwkv_w_g2, rwkv_k_k, rwkv_k_a, rwkv_r_k, rwkv_ln_g, rwkv_ln_b, sb_q_g, sb_k_g, sb_bias, ffn_w1, ffn_w3, ffn_w2, moe_router, moe_w1, moe_w3, moe_w2):
    depth = w_in.shape[0]
    bp, seq, d = x_prompt.shape
    bs, dec, _ = x_sample.shape
    n_pool, page = cache_sb_k.shape[1:3]

    g0 = 0
    gq, gk, gv = w_in[:, :, g0:g0 + GLA_QK], w_in[:, :, g0 + GLA_QK:g0 + 2 * GLA_QK], w_in[:, :, g0 + 2 * GLA_QK:g0 + 2 * GLA_QK + GLA_V]
    g_low = w_in[:, :, g0 + 2 * GLA_QK + GLA_V:g0 + 2 * GLA_QK + GLA_V + GLA_LORA]
    g_r = w_in[:, :, g0 + 2 * GLA_QK + GLA_V + GLA_LORA:g0 + 2 * GLA_QK + 2 * GLA_V + GLA_LORA]
    gla_cols = 2 * GLA_QK + 2 * GLA_V + GLA_LORA
    g_low = jnp.pad(g_low, ((0, 0), (0, 0), (0, LANES - GLA_LORA)))
    w_in_r = jnp.concatenate([gq, gk, gv, g_r, g_low, w_in[:, :, gla_cols:]], axis=-1).astype(BF16)

    tile_heads = lambda g, h: jnp.tile(g, (1, h))
    W = dict(
        norm1_g=norm1_g, norm2_g=norm2_g, w_in=w_in_r,
        wo_g=w_out[:, :GLA_V].astype(BF16), wo_r=w_out[:, GLA_V:GLA_V + RWKV_W].astype(BF16),
        wo_s=w_out[:, GLA_V + RWKV_W:].astype(BF16),
        gla_wa2=_pad_rows(gla_w_a2, LANES, 0), gla_ba=gla_b_a[:, None, :],
        gla_ng=tile_heads(gla_norm_g, GLA_HEADS)[:, None, :],
        rwkv=dict(
            mu=rwkv_mu[:, None, :], w0=rwkv_w0[:, None, :], ww2=_pad_rows(rwkv_w_w2, LANES, 0),
            a0=rwkv_a0[:, None, :], wa2=_pad_rows(rwkv_w_a2, LANES, RWKV_W_LORA), wg2=rwkv_w_g2,
            kk=rwkv_k_k[:, None, :], ka=rwkv_k_a[:, None, :], rk=rwkv_r_k.reshape(depth, 1, RWKV_W),
            lng=rwkv_ln_g[:, None, :], lnb=rwkv_ln_b[:, None, :]),
        sb_qg=tile_heads(sb_q_g, SB_HEADS)[:, None, :], sb_kg=tile_heads(sb_k_g, SB_HEADS)[:, None, :],
        ffn_w1=ffn_w1.astype(BF16), ffn_w3=ffn_w3.astype(BF16), ffn_w2=ffn_w2.astype(BF16),
        router=jnp.pad(moe_router, ((0, 0), (0, 0), (0, LANES - moe_router.shape[-1]))),
        moe_w1=moe_w1.astype(BF16), moe_w3=moe_w3.astype(BF16), moe_w2=moe_w2.astype(BF16),
    )

    rows = bp + bs
    rows_pad = -(-rows // 8) * 8
    c_all = jnp.pad(jnp.concatenate([c_prompt, c_sample], axis=0), ((0, rows_pad - rows), (0, 0)))
    mod_all = _modulation(c_all, w_ada, b_ada)

    xp = x_prompt.reshape(bp * seq, d)
    xs = x_sample.reshape(bs * dec, d)
    cache_k = jnp.transpose(cache_sb_k, (0, 1, 3, 4, 2))
    cache_v = jnp.transpose(cache_sb_v, (0, 1, 3, 4, 2))
    zero_p = (jnp.zeros((bp, GLA_V, GLA_QK), F32), jnp.zeros((bp, HEAD_DIM, RWKV_W), F32),
              jnp.zeros((bp, 1, RWKV_COLS), F32))
    st_p, st_s = [], []
    for l in range(depth):
        mod_p = mod_all[l, :bp][:, None, :]
        mod_s = jnp.repeat(mod_all[l, bp:rows], dec, axis=0)[None]
        xp, sp = _layer(l, xp, mod_p, bp, zero_p, W,
                        lambda q, kb, vb, k_rows, v_rows: _sb_prompt(q, kb, vb, bp), sb_bias[l])
        state_s = (_gla_state_to_bd(state_gla[l]), _rwkv_state_in(state_rwkv[l]), state_shift[l][:, None, :])
        xs, ss = _layer(l, xs, mod_s, bs, state_s, W,
                        lambda q, kb, vb, k_rows, v_rows: _sb_sample(q, k_rows, v_rows, cache_k, cache_v, l,
                                                                     page_table, sb_bias[l], bs))
        st_p.append(sp)
        st_s.append(ss)

    def stack(sts, idx, shape):
        return jnp.stack([s[idx] for s in sts]).reshape((depth,) + shape)

    outs = [xp.reshape(bp, seq, d), xs.reshape(bs, dec, d)]
    for sts, b, t in ((st_p, bp, seq), (st_s, bs, dec)):
        outs += [
            stack(sts, 0, (b, t, SB_HEADS, HEAD_DIM)), stack(sts, 1, (b, t, SB_HEADS, HEAD_DIM)),
            stack(sts, 2, (b, GLA_HEADS, GLA_DK, GLA_DV)), stack(sts, 3, (b, RWKV_HEADS, HEAD_DIM, HEAD_DIM)),
            stack(sts, 4, (b, RWKV_COLS)),
        ]
    return tuple(outs)
```

```python
import functools
import math

import numpy as np
import jax
import jax.numpy as jnp
from jax import lax
from jax.experimental import pallas as pl
from jax.experimental.pallas import tpu as pltpu

F32 = jnp.float32
BF16 = jnp.bfloat16
HIGHEST = lax.Precision.HIGHEST

HEAD_DIM = 64
GLA_HEADS = 4
GLA_DK = 32
GLA_DV = 64
GLA_LORA = 16
GLA_TAU = 16.0
GLA_CHUNK = 16
RWKV_HEADS = 4
RWKV_W_LORA = 64
RWKV_A_LORA = 64
RWKV_G_LORA = 128
RWKV_LN_EPS = 64e-5
SB_HEADS = 8
TOP_K = 2
EPS = 1e-6
LOG2E = math.log2(math.e)
SB_DEAD_BITS = 160.0

GLA_QK = GLA_HEADS * GLA_DK
GLA_V = GLA_HEADS * GLA_DV
GLA_PCOLS = 2 * GLA_QK + 2 * GLA_V + 128
RWKV_W = RWKV_HEADS * HEAD_DIM
RWKV_COLS = 3 * RWKV_W + RWKV_W_LORA + RWKV_A_LORA + RWKV_G_LORA
SB_W = SB_HEADS * HEAD_DIM
SB_COLS = 3 * SB_W
LANES = 128
PAGE = 128
VMEM_LIMIT = 56 * 1024 * 1024


def _cparams(sem):
    return pltpu.CompilerParams(dimension_semantics=sem, vmem_limit_bytes=VMEM_LIMIT)


def _tile(n, pref):
    if n <= pref:
        return n
    t = pref
    while n % t:
        t -= 8
    return t


def _block_ones(n, group, value=1.0):
    idx = np.arange(n) // group
    return jnp.asarray((idx[:, None] == idx[None, :]).astype(np.float32) * value)


def _dot(a, b):
    return jnp.dot(a, b, preferred_element_type=F32)


def _dot_hi(a, b):
    return jnp.dot(a, b, preferred_element_type=F32, precision=HIGHEST)


def _dot_nt(a, b, precision=None):
    return lax.dot_general(a, b, (((1,), (1,)), ((), ())), preferred_element_type=F32, precision=precision)


def _dot_tn(a, b, precision=None):
    return lax.dot_general(a, b, (((0,), (0,)), ((), ())), preferred_element_type=F32, precision=precision)


def _dot_split(x, m_bf16):
    hi = x.astype(BF16)
    lo = (x - hi.astype(F32)).astype(BF16)
    return _dot(hi, m_bf16) + _dot(lo, m_bf16)


def _sigmoid(x):
    return 1.0 / (1.0 + jnp.exp(-x))


def _softplus(x):
    return jnp.maximum(x, 0.0) + jnp.log(1.0 + jnp.exp(-jnp.abs(x)))


def _mod_kernel(c_ref, w_ref, b_ref, o_ref):
    c = c_ref[...]
    o_ref[0] = _dot_hi(c * _sigmoid(c), w_ref[0]) + b_ref[0]


def _modulation(c_all, w_ada, b_ada):
    depth, d, d6 = w_ada.shape
    r = c_all.shape[0]
    tn = 1024
    return pl.pallas_call(
        _mod_kernel,
        grid=(depth, d6 // tn),
        in_specs=[
            pl.BlockSpec((r, d), lambda l, j: (0, 0)),
            pl.BlockSpec((1, d, tn), lambda l, j: (l, 0, j)),
            pl.BlockSpec((1, 1, tn), lambda l, j: (l, 0, j)),
        ],
        out_specs=pl.BlockSpec((1, r, tn), lambda l, j: (l, 0, j)),
        out_shape=jax.ShapeDtypeStruct((depth, r, d6), F32),
        compiler_params=_cparams(("arbitrary", "arbitrary")),
    )(c_all, w_ada, b_ada.reshape(depth, 1, d6))


def _mod_spec(mod, tm, tpb, chunk):
    d = mod.shape[2] // 6
    if mod.shape[1] == 1:
        return pl.BlockSpec((1, 1, d), lambda i, *_: (i // tpb, 0, chunk))
    return pl.BlockSpec((1, tm, d), lambda i, *_: (i // tpb, i % tpb, chunk))


def _inproj_kernel(x_ref, sh_ref, sc_ref, g_ref, w_ref, pg_ref, pr_ref, ps_ref):
    x = x_ref[...]
    ms = jnp.mean(x * x, axis=-1, keepdims=True)
    h = x * lax.rsqrt(ms + EPS) * g_ref[...]
    h = h * (1.0 + sc_ref[0]) + sh_ref[0]
    p = _dot(h.astype(BF16), w_ref[...])
    pg_ref[...] = p[:, :GLA_PCOLS]
    pr_ref[...] = p[:, GLA_PCOLS:GLA_PCOLS + RWKV_COLS]
    ps_ref[...] = p[:, GLA_PCOLS + RWKV_COLS:]


def _inproj(x, mod, norm_g, w):
    n, d = x.shape
    nb = mod.shape[0]
    tm = _tile(n // nb, 256)
    tpb = (n // nb) // tm
    ncols = w.shape[1]
    return pl.pallas_call(
        _inproj_kernel,
        grid=(n // tm,),
        in_specs=[
            pl.BlockSpec((tm, d), lambda i: (i, 0)),
            _mod_spec(mod, tm, tpb, 0),
            _mod_spec(mod, tm, tpb, 1),
            pl.BlockSpec((1, d), lambda i: (0, 0)),
            pl.BlockSpec((d, ncols), lambda i: (0, 0)),
        ],
        out_specs=[
            pl.BlockSpec((tm, GLA_PCOLS), lambda i: (i, 0)),
            pl.BlockSpec((tm, RWKV_COLS), lambda i: (i, 0)),
            pl.BlockSpec((tm, SB_COLS), lambda i: (i, 0)),
        ],
        out_shape=[
            jax.ShapeDtypeStruct((n, GLA_PCOLS), F32),
            jax.ShapeDtypeStruct((n, RWKV_COLS), F32),
            jax.ShapeDtypeStruct((n, SB_COLS), F32),
        ],
        compiler_params=_cparams(("arbitrary",)),
    )(x, mod, mod, norm_g.reshape(1, d), w)


def _gla_kernel(p_ref, s0_ref, wa2_ref, ba_ref, ng_ref, hexp_ref, havg_ref, bd_ref, tril_ref, ones_ref,
                o_ref, s_ref, b_scr, qin_scr, kout_scr, dec_scr, o_scr, *, chunk, n_chunks):
    c_len = chunk

    @pl.when(pl.program_id(1) == 0)
    def _():
        s_ref[0] = s0_ref[0]

    glow = p_ref[0, :, 2 * GLA_QK + 2 * GLA_V:]
    x = _dot_hi(glow, wa2_ref[...]) + ba_ref[...]
    la = -_softplus(-x) * (1.0 / GLA_TAU)
    la_hi = la.astype(BF16)
    la_lo = (la - la_hi.astype(F32)).astype(BF16)
    b = _dot(tril_ref[...], la_hi) + _dot(tril_ref[...], la_lo)
    b_tot = _dot(ones_ref[...], la_hi) + _dot(ones_ref[...], la_lo)
    b_scr[...] = b
    qin_scr[...] = p_ref[0, :, 0:GLA_QK] * (GLA_DK ** -0.5) * jnp.exp(b)
    kout_scr[...] = p_ref[0, :, GLA_QK:2 * GLA_QK] * jnp.exp(b_tot - b)
    dec_scr[...] = jnp.exp(b_tot)
    row = lax.broadcasted_iota(jnp.int32, (c_len, 1), 0)
    hexp = hexp_ref[...]
    bd = bd_ref[...]

    def body(c, carry):
        sl = pl.ds(pl.multiple_of(c * c_len, c_len), c_len)
        q = p_ref[0, sl, 0:GLA_QK] * (GLA_DK ** -0.5)
        k = p_ref[0, sl, GLA_QK:2 * GLA_QK]
        v = p_ref[0, sl, 2 * GLA_QK:2 * GLA_QK + GLA_V]
        b = b_scr[sl, :]
        zs = []
        for s in range(c_len):
            m = row >= s
            e = jnp.exp(jnp.where(m, b - b[s:s + 1, :], 0.0))
            zs.append(jnp.where(m, q * k[s:s + 1, :] * e, 0.0))
        a_exp = _dot_split(jnp.concatenate(zs, axis=0), hexp)
        o = jnp.zeros((c_len, GLA_V), F32)
        for s in range(c_len):
            o = o + a_exp[s * c_len:(s + 1) * c_len, :] * v[s:s + 1, :]
        st = s_ref[0]
        o_scr[sl, :] = o + _dot_nt(qin_scr[sl, :], st, HIGHEST)
        s_ref[0] = st * dec_scr[pl.ds(c * c_len, 1), :] + _dot_tn(v, kout_scr[sl, :], HIGHEST) * bd
        return carry

    lax.fori_loop(0, n_chunks, body, 0, unroll=min(4, n_chunks))
    o = o_scr[...]
    r = p_ref[0, :, 2 * GLA_QK + GLA_V:2 * GLA_QK + 2 * GLA_V]
    on = o * lax.rsqrt(_dot_split(o * o, havg_ref[...]) + EPS) * ng_ref[...]
    o_ref[0] = (on * (r * _sigmoid(r))).astype(o_ref.dtype)


def _gla(p_gla, s0t, wa2p, ba, ng_t, batch, chunk):
    n = p_gla.shape[0]
    t = n // batch
    tt = _tile(t, 512)
    n_chunks = tt // chunk
    p3 = p_gla.reshape(batch, t, GLA_PCOLS)
    hexp = jnp.asarray((np.arange(GLA_QK)[:, None] // GLA_DK == np.arange(GLA_V)[None, :] // GLA_DV).astype(np.float32))
    havg = _block_ones(GLA_V, GLA_DV, 1.0 / GLA_DV)
    bd = jnp.asarray((np.arange(GLA_V)[:, None] // GLA_DV == np.arange(GLA_QK)[None, :] // GLA_DK).astype(np.float32))
    rows = np.arange(tt)
    same = rows[:, None] // chunk == rows[None, :] // chunk
    tril = jnp.asarray((same & (rows[:, None] >= rows[None, :])).astype(np.float32)).astype(BF16)
    ones = jnp.asarray(same.astype(np.float32)).astype(BF16)
    const = lambda shape: pl.BlockSpec(shape, lambda b, i: (0,) * len(shape))
    scr = lambda w: pltpu.VMEM((tt, w), F32)
    o, s = pl.pallas_call(
        functools.partial(_gla_kernel, chunk=chunk, n_chunks=n_chunks),
        grid=(batch, t // tt),
        in_specs=[
            pl.BlockSpec((1, tt, GLA_PCOLS), lambda b, i: (b, i, 0)),
            pl.BlockSpec((1, GLA_V, GLA_QK), lambda b, i: (b, 0, 0)),
            const((LANES, GLA_QK)), const((1, GLA_QK)), const((1, GLA_V)),
            const((GLA_QK, GLA_V)), const((GLA_V, GLA_V)), const((GLA_V, GLA_QK)), const((tt, tt)), const((tt, tt)),
        ],
        out_specs=[
            pl.BlockSpec((1, tt, GLA_V), lambda b, i: (b, i, 0)),
            pl.BlockSpec((1, GLA_V, GLA_QK), lambda b, i: (b, 0, 0)),
        ],
        out_shape=[
            jax.ShapeDtypeStruct((batch, t, GLA_V), BF16),
            jax.ShapeDtypeStruct((batch, GLA_V, GLA_QK), F32),
        ],
        scratch_shapes=[scr(GLA_QK), scr(GLA_QK), scr(GLA_QK), scr(GLA_QK), scr(GLA_V)],
        compiler_params=_cparams(("arbitrary", "arbitrary")),
    )(p3, s0t, wa2p, ba, ng_t, hexp.astype(BF16), havg.astype(BF16), bd, tril, ones)
    return o.reshape(n, GLA_V), s


def _gla_state_to_bd(s0):
    b = s0.shape[0]
    eye = jnp.eye(GLA_HEADS, dtype=s0.dtype)
    st = jnp.einsum('bhkv,hg->bhvgk', s0, eye)
    return st.reshape(b, GLA_V, GLA_QK)


def _gla_state_from_bd(st):
    b = st.shape[0]
    s5 = st.reshape(b, GLA_HEADS, GLA_DV, GLA_HEADS, GLA_DK)
    diag = jnp.stack([s5[:, h, :, h, :] for h in range(GLA_HEADS)], axis=1)
    return jnp.swapaxes(diag, 2, 3)


def _rwkv_kernel(p_ref, sh0_ref, s0_ref, mu_ref, w0_ref, ww2_ref, a0_ref, wa2_ref, wg2_ref, kk_ref, ka_ref,
                 rk_ref, lng_ref, lnb_ref, ones_ref, onesb_ref, avg_ref, sel_ref,
                 o_ref, s_ref, prev_scr, nkk_scr, d_scr, b_scr, k_scr, r_scr, v_scr, y_scr, *, nbb, tt):
    first = pl.program_id(1) == 0

    @pl.when(first)
    def _():
        s_ref[...] = s0_ref[...]
        prev_scr[...] = sh0_ref[...]

    ones_f = ones_ref[...]
    ones_b = onesb_ref[...]
    avg = avg_ref[...]
    sel = sel_ref[...]
    row = lax.broadcasted_iota(jnp.int32, (tt, 1), 0)
    gate = []
    for bb in range(nbb):
        p = p_ref[bb]
        p_prev = jnp.where(row == 0, prev_scr[bb], pltpu.roll(p, 1, axis=0))
        prev_scr[bb] = p[tt - 1:tt, :]
        xm = p + (p_prev - p) * mu_ref[...]
        r = xm[:, 0:RWKV_W]
        k = xm[:, RWKV_W:2 * RWKV_W]
        v = xm[:, 2 * RWKV_W:3 * RWKV_W]
        xwa = xm[:, 3 * RWKV_W:3 * RWKV_W + LANES]
        xg = xm[:, 3 * RWKV_W + LANES:]
        w = -_softplus(-(w0_ref[...] + _dot_hi(jnp.tanh(xwa), ww2_ref[...]))) - 0.5
        a = _sigmoid(a0_ref[...] + _dot_hi(xwa, wa2_ref[...]))
        gate.append(_dot_hi(_sigmoid(xg), wg2_ref[...]))
        kk = k * kk_ref[...]
        nrm = jnp.sqrt(_dot_hi(kk * kk, ones_f))
        kk = kk / jnp.maximum(nrm, 1e-12)
        k2 = k * (1.0 + (a - 1.0) * ka_ref[...])
        nkk_scr[bb] = -kk
        d_scr[bb] = jnp.exp(-jnp.exp(w))
        b_scr[bb] = kk * a
        k_scr[bb] = k2
        r_scr[bb] = r
        v_scr[bb] = v

    def head_sums(xs, split):
        stacked = jnp.concatenate([x[:, half * LANES:(half + 1) * LANES] for x in xs for half in range(2)], axis=0)
        res = _dot_split(stacked, ones_b) if split else _dot(stacked.astype(BF16), ones_b)
        out, row = [], 0
        for x in xs:
            n_rows = x.shape[0]
            out.append(jnp.concatenate([res[row:row + n_rows], res[row + n_rows:row + 2 * n_rows]], axis=1))
            row += 2 * n_rows
        return out

    lane_lo = lax.broadcasted_iota(jnp.int32, (1, LANES), 1) < HEAD_DIM

    def seg_sums(x):
        halves = []
        for half in range(2):
            xh = x[:, half * LANES:(half + 1) * LANES]
            lo = jnp.sum(jnp.where(lane_lo, xh, 0.0), axis=1, keepdims=True)
            hi = jnp.sum(jnp.where(lane_lo, 0.0, xh), axis=1, keepdims=True)
            halves.append(jnp.where(lane_lo, lo, hi))
        return jnp.concatenate(halves, axis=1)

    def read_out(s, bb, ts):
        yb, = head_sums([s * jnp.broadcast_to(r_scr[bb, ts, :], (HEAD_DIM, RWKV_W))], False)
        y_scr[bb, ts, :] = jnp.sum(yb * sel, axis=0, keepdims=True)

    def step(t, states):
        ts = pl.ds(t, 1)
        tp = pl.ds(jnp.maximum(t - 1, 0), 1)
        bc = lambda ref, bb, rows: jnp.broadcast_to(ref[bb, rows, :], (HEAD_DIM, RWKV_W))
        sas = [seg_sums(states[bb] * bc(nkk_scr, bb, ts)) for bb in range(nbb)]
        side = head_sums([jnp.concatenate([states[bb] * bc(r_scr, bb, tp), sel * bc(v_scr, bb, ts)], axis=0)
                          for bb in range(nbb)], False)
        new = []
        for bb in range(nbb):
            y_scr[bb, tp, :] = jnp.sum(side[bb][0:HEAD_DIM] * sel, axis=0, keepdims=True)
            vb = side[bb][HEAD_DIM:]
            new.append(states[bb] * bc(d_scr, bb, ts) + sas[bb] * bc(b_scr, bb, ts) + vb * bc(k_scr, bb, ts))
        return tuple(new)

    states = lax.fori_loop(0, tt, step, tuple(s_ref[bb] for bb in range(nbb)))
    for bb in range(nbb):
        s_ref[bb] = states[bb]
        read_out(states[bb], bb, pl.ds(tt - 1, 1))
        y = y_scr[bb]
        mean = _dot_hi(y, avg)
        yc = y - mean
        var = _dot_hi(yc * yc, avg)
        yn = yc * lax.rsqrt(var + RWKV_LN_EPS) * lng_ref[...] + lnb_ref[...]
        bonus = _dot_hi(r_scr[bb] * k_scr[bb] * rk_ref[...], ones_f) * v_scr[bb]
        o_ref[bb] = ((yn + bonus) * gate[bb]).astype(o_ref.dtype)


def _rwkv(p_rwkv, shift0, s0, wl, batch, nbb):
    n = p_rwkv.shape[0]
    t = n // batch
    tt = _tile(t, 256)
    p3 = p_rwkv.reshape(batch, t, RWKV_COLS)
    ones_f = _block_ones(RWKV_W, HEAD_DIM)
    avg = _block_ones(RWKV_W, HEAD_DIM, 1.0 / HEAD_DIM)
    sel = jnp.asarray((np.arange(HEAD_DIM)[:, None] == np.arange(RWKV_W)[None, :] % HEAD_DIM).astype(np.float32))
    const = lambda shape: pl.BlockSpec(shape, lambda b, i: (0,) * len(shape))
    row = const((1, RWKV_W))
    scr = lambda: pltpu.VMEM((nbb, tt, RWKV_W), F32)
    o, s = pl.pallas_call(
        functools.partial(_rwkv_kernel, nbb=nbb, tt=tt),
        grid=(batch // nbb, t // tt),
        in_specs=[
            pl.BlockSpec((nbb, tt, RWKV_COLS), lambda b, i: (b, i, 0)),
            pl.BlockSpec((nbb, 1, RWKV_COLS), lambda b, i: (b, 0, 0)),
            pl.BlockSpec((nbb, HEAD_DIM, RWKV_W), lambda b, i: (b, 0, 0)),
            const((1, RWKV_COLS)), row, const((LANES, RWKV_W)), row, const((LANES, RWKV_W)),
            const((RWKV_G_LORA, RWKV_W)), row, row, row, row, row,
            const((RWKV_W, RWKV_W)), const((LANES, LANES)), const((RWKV_W, RWKV_W)), const((HEAD_DIM, RWKV_W)),
        ],
        out_specs=[
            pl.BlockSpec((nbb, tt, RWKV_W), lambda b, i: (b, i, 0)),
            pl.BlockSpec((nbb, HEAD_DIM, RWKV_W), lambda b, i: (b, 0, 0)),
        ],
        out_shape=[
            jax.ShapeDtypeStruct((batch, t, RWKV_W), BF16),
            jax.ShapeDtypeStruct((batch, HEAD_DIM, RWKV_W), F32),
        ],
        scratch_shapes=[pltpu.VMEM((nbb, 1, RWKV_COLS), F32)] + [scr() for _ in range(7)],
        compiler_params=_cparams(("arbitrary", "arbitrary")),
    )(p3, shift0, s0, wl['mu'], wl['w0'], wl['ww2'], wl['a0'], wl['wa2'], wl['wg2'], wl['kk'], wl['ka'],
      wl['rk'], wl['lng'], wl['lnb'], ones_f, _block_ones(LANES, HEAD_DIM).astype(BF16), avg, sel)
    return o.reshape(n, RWKV_W), s


def _rwkv_state_in(s0):
    b = s0.shape[0]
    return jnp.transpose(s0, (0, 2, 1, 3)).reshape(b, HEAD_DIM, RWKV_W)


def _rwkv_state_out(s):
    b = s.shape[0]
    return jnp.transpose(s.reshape(b, HEAD_DIM, RWKV_HEADS, HEAD_DIM), (0, 2, 1, 3))


def _sbprep_kernel(p_ref, qg_ref, kg_ref, avg_ref, *rest, aug):
    if aug:
        qrow_ref, krow_ref, q_ref, k_ref, kb_ref, vb_ref = rest
    else:
        q_ref, k_ref, kb_ref, vb_ref = rest
    avg = avg_ref[...]
    q = p_ref[:, 0:SB_W]
    k = p_ref[:, SB_W:2 * SB_W]
    qn = q * lax.rsqrt(_dot_hi(q * q, avg) + EPS) * (qg_ref[...] * (HEAD_DIM ** -0.5 * LOG2E))
    kn = k * lax.rsqrt(_dot_hi(k * k, avg) + EPS) * kg_ref[...]
    k_ref[...] = kn
    vb_ref[...] = p_ref[:, 2 * SB_W:].astype(BF16)
    if aug:
        lo = lax.broadcasted_iota(jnp.int32, (1, LANES), 1) < HEAD_DIM

        def widen(x, row):
            parts = []
            for h in range(SB_HEADS):
                blk = x[:, (h // 2) * LANES:(h // 2 + 1) * LANES]
                if h % 2:
                    blk = pltpu.roll(blk, HEAD_DIM, axis=1)
                parts.append(jnp.where(lo, blk, 0.0))
            return (jnp.concatenate(parts, axis=1) + row).astype(BF16)

        q_ref[...] = widen(qn, qrow_ref[...])
        kb_ref[...] = widen(kn, krow_ref[...])
    else:
        q_ref[...] = qn.astype(BF16)
        kb_ref[...] = kn.astype(BF16)


def _sbprep(p_sb, qg_t, kg_t, bias=None):
    n = p_sb.shape[0]
    tm = _tile(n, 512)
    aug = bias is not None
    avg = _block_ones(SB_W, HEAD_DIM, 1.0 / HEAD_DIM)
    wide = SB_HEADS * LANES if aug else SB_W
    out = lambda w: pl.BlockSpec((tm, w), lambda i: (i, 0))
    const = lambda shape: pl.BlockSpec(shape, lambda i: (0,) * len(shape))
    in_specs = [pl.BlockSpec((tm, SB_COLS), lambda i: (i, 0)), const((1, SB_W)), const((1, SB_W)), const((SB_W, SB_W))]
    args = [p_sb, qg_t, kg_t, avg]
    if aug:
        b2 = bias * LOG2E
        hi = b2.astype(BF16).astype(F32)
        lo = (b2 - hi).astype(BF16).astype(F32)
        qrow = jnp.zeros((SB_HEADS, LANES), F32).at[:, HEAD_DIM].set(hi).at[:, HEAD_DIM + 1].set(lo)
        krow = jnp.zeros((SB_HEADS, LANES), F32).at[:, HEAD_DIM:HEAD_DIM + 2].set(1.0)
        in_specs += [const((1, wide)), const((1, wide))]
        args += [qrow.reshape(1, wide), krow.reshape(1, wide)]
    return pl.pallas_call(
        functools.partial(_sbprep_kernel, aug=aug),
        grid=(n // tm,),
        in_specs=in_specs,
        out_specs=[out(wide), out(SB_W), out(wide), out(SB_W)],
        out_shape=[
            jax.ShapeDtypeStruct((n, wide), BF16),
            jax.ShapeDtypeStruct((n, SB_W), F32),
            jax.ShapeDtypeStruct((n, wide), BF16),
            jax.ShapeDtypeStruct((n, SB_W), BF16),
        ],
        compiler_params=_cparams(("arbitrary",)),
    )(*args)


def _sb_keep(z, mask):
    keep = jnp.maximum(z, 0.0) + jnp.log2(1.0 + jnp.exp2(-jnp.abs(z)))
    if mask is not None:
        keep = jnp.where(mask, keep, 0.0)
    return z - keep, keep.astype(BF16)


def _sb_logits_stage(z, mask, cum):
    zk, kb = _sb_keep(z, mask)
    tk = z.shape[1]
    cs = _dot(kb, cum)
    return zk - cs[:, :tk], cs[:, tk:]


def _sb_weights(part, r_run, mask):
    reps = part.shape[1] // r_run.shape[1]
    e = jnp.exp2(part - (jnp.concatenate([r_run] * reps, axis=1) if reps > 1 else r_run))
    if mask is not None:
        e = jnp.where(mask, e, 0.0)
    return e


def _sb_prompt_kernel(q_ref, k_ref, v_ref, later_ref, o_ref, zk_scr, kb_scr, tot_scr, *, tq):
    i = pl.program_id(2)
    later = later_ref[...]
    rows = lax.broadcasted_iota(jnp.int32, (tq, tq), 0)
    cols = lax.broadcasted_iota(jnp.int32, (tq, tq), 1)
    diag = cols < rows

    def stage1(j, mask):
        ks = pl.ds(pl.multiple_of(j * tq, tq), tq)
        out = []
        for u in range(2):
            z = _dot_nt(q_ref[:, u * LANES:(u + 1) * LANES], k_ref[ks, u * LANES:(u + 1) * LANES])
            keep = jnp.maximum(z, 0.0) + jnp.log2(1.0 + jnp.exp2(-jnp.abs(z)))
            if mask is not None:
                keep = jnp.where(mask, keep, 0.0)
            tot = jnp.broadcast_to(jnp.sum(keep, axis=1, keepdims=True), (tq, LANES))
            out.append((z - keep, keep.astype(BF16), tot))
        return out

    def stage2(j, pend, carry, mask, valid):
        vb = v_ref[pl.ds(pl.multiple_of(j * tq, tq), tq), :]
        out = []
        for u in range(2):
            zk, kb, tot = pend[u]
            acc, r_run = carry[u]
            pv = _dot(_sb_weights(zk - _dot(kb, later), r_run, mask).astype(BF16), vb)
            if valid is not None:
                pv = jnp.where(valid, pv, 0.0)
                tot = jnp.where(valid, tot, 0.0)
            out.append((acc + pv, r_run + tot))
        return tuple(out)

    carry = tuple((jnp.zeros((tq, LANES), F32), jnp.zeros((tq, LANES), F32)) for _ in range(2))
    carry = stage2(i, stage1(i, diag), carry, diag, None)
    zk_scr[...] = jnp.zeros_like(zk_scr)
    kb_scr[...] = jnp.zeros_like(kb_scr)
    tot_scr[...] = jnp.zeros_like(tot_scr)

    def half_trip(n, slot, carry):
        nxt = stage1(jnp.maximum(i - 1 - n, 0), None)
        pend = [(zk_scr[slot, u], kb_scr[slot, u], tot_scr[slot, u]) for u in range(2)]
        carry = stage2(jnp.maximum(i - n, 0), pend, carry, None, (n >= 1) & (n <= i))
        for u in range(2):
            zk_scr[1 - slot, u] = nxt[u][0]
            kb_scr[1 - slot, u] = nxt[u][1]
            tot_scr[1 - slot, u] = nxt[u][2]
        return carry

    def body(state):
        m, _, carry = state
        live = jnp.minimum(jnp.min(carry[0][1]), jnp.min(carry[1][1])) < SB_DEAD_BITS
        carry = half_trip(2 * m, 0, carry)
        carry = half_trip(2 * m + 1, 1, carry)
        return m + 1, live, carry

    _, _, carry = lax.while_loop(lambda s: (2 * s[0] < i + 1) & s[1], body, (jnp.int32(0), jnp.bool_(True), carry))
    lo = lax.broadcasted_iota(jnp.int32, (1, LANES), 1) < HEAD_DIM
    o_ref[...] = jnp.where(lo, carry[0][0], carry[1][0]).astype(o_ref.dtype)


def _cum_matrix(tk):
    j = np.arange(tk)
    later = (j[:, None] > j[None, :]).astype(np.float32)
    return jnp.asarray(np.concatenate([later, np.ones((tk, LANES), np.float32)], axis=1)).astype(BF16)


def _sb_prompt(q_aug, k_aug, vb, batch):
    n = vb.shape[0]
    t = n // batch
    tq = _tile(t, 256)
    tpb = t // tq
    return pl.pallas_call(
        functools.partial(_sb_prompt_kernel, tq=tq),
        grid=(batch, SB_HEADS // 2, tpb),
        in_specs=[
            pl.BlockSpec((tq, 2 * LANES), lambda b, h, i: (b * tpb + i, h)),
            pl.BlockSpec((t, 2 * LANES), lambda b, h, i: (b, h)),
            pl.BlockSpec((t, LANES), lambda b, h, i: (b, h)),
            pl.BlockSpec((tq, tq), lambda b, h, i: (0, 0)),
        ],
        out_specs=pl.BlockSpec((tq, LANES), lambda b, h, i: (b * tpb + i, h)),
        out_shape=jax.ShapeDtypeStruct((n, SB_W), BF16),
        scratch_shapes=[pltpu.VMEM((2, 2, tq, tq), F32), pltpu.VMEM((2, 2, tq, tq), BF16),
                        pltpu.VMEM((2, 2, tq, LANES), F32)],
        compiler_params=_cparams(("arbitrary", "arbitrary", "arbitrary")),
    )(q_aug, k_aug, vb, _cum_matrix(tq)[:, :tq])


def _sb_sample_kernel(pt_ref, q_ref, kn_ref, vn_ref, bias_ref, bd_ref, cum_ref, *rest, n_pages, ppb, dec):
    k_refs = rest[:ppb]
    v_refs = rest[ppb:2 * ppb]
    o_ref, qbd_scr, acc_scr, run_scr = rest[2 * ppb:]
    g = pl.program_id(1)
    rows_n = SB_HEADS * dec
    bd = bd_ref[...]
    cum = cum_ref[...]
    bias = bias_ref[...]

    def logits(kt, mask):
        return _sb_logits_stage(_dot(qbd_scr[...], kt.astype(BF16)) + bias, mask, cum)

    def attend(kt, vt, mask):
        part, tot = logits(kt, mask)
        acc_scr[...] += _dot_nt(_sb_weights(part, run_scr[...], mask).astype(BF16), vt.astype(BF16))
        run_scr[...] += tot

    @pl.when(g == 0)
    def _():
        q = q_ref[0].astype(F32)
        qbd_scr[...] = (jnp.concatenate([q] * SB_HEADS, axis=0) * bd).astype(BF16)
        acc_scr[...] = jnp.zeros_like(acc_scr)
        run_scr[...] = jnp.zeros_like(run_scr)
        qi = lax.broadcasted_iota(jnp.int32, (rows_n, PAGE), 0) % dec
        kj = lax.broadcasted_iota(jnp.int32, (rows_n, PAGE), 1)
        attend(kn_ref[0], vn_ref[0], kj < qi)

    pend = [logits(k_refs[pidx][0, 0].reshape(SB_W, PAGE), None) for pidx in range(ppb)]
    r_run = run_scr[...]
    acc = acc_scr[...]
    for pidx in range(ppb):
        part, tot = pend[pidx]
        vt = v_refs[pidx][0, 0].reshape(SB_W, PAGE).astype(BF16)
        acc = acc + _dot_nt(_sb_weights(part, r_run, None).astype(BF16), vt)
        r_run = r_run + tot
    run_scr[...] = r_run
    acc_scr[...] = acc

    @pl.when(g == n_pages // ppb - 1)
    def _():
        masked = acc_scr[...] * bd
        o = masked[0:dec, :]
        for h in range(1, SB_HEADS):
            o = o + masked[h * dec:(h + 1) * dec, :]
        o_ref[0] = o.astype(o_ref.dtype)


def _sb_sample(qs, k_rows, v_rows, cache_k, cache_v, layer, page_table, bias, batch):
    n = qs.shape[0]
    dec = n // batch
    n_pages = page_table.shape[1]
    ppb = 8 if n_pages % 8 == 0 else 1
    rows_n = SB_HEADS * dec
    q3 = qs.reshape(batch, dec, SB_W)

    def as_page(a):
        a = jnp.swapaxes(a.reshape(batch, dec, SB_W), 1, 2)
        return jnp.pad(a, ((0, 0), (0, 0), (0, PAGE - dec)))

    bias_rows = jnp.broadcast_to(jnp.repeat(bias * LOG2E, dec)[:, None], (rows_n, PAGE))
    bd = jnp.asarray((np.arange(rows_n)[:, None] // dec == np.arange(SB_W)[None, :] // HEAD_DIM).astype(np.float32))

    def page_spec(pidx):
        return pl.BlockSpec((1, 1, SB_HEADS, HEAD_DIM, PAGE),
                            lambda b, g, pt: (layer, pt[b, n_pages - 1 - (g * ppb + pidx)], 0, 0, 0))

    const = lambda shape: pl.BlockSpec(shape, lambda b, g, pt: (0,) * len(shape))
    grid_spec = pltpu.PrefetchScalarGridSpec(
        num_scalar_prefetch=1,
        grid=(batch, n_pages // ppb),
        in_specs=[
            pl.BlockSpec((1, dec, SB_W), lambda b, g, pt: (b, 0, 0)),
            pl.BlockSpec((1, SB_W, PAGE), lambda b, g, pt: (b, 0, 0)),
            pl.BlockSpec((1, SB_W, PAGE), lambda b, g, pt: (b, 0, 0)),
            const((rows_n, PAGE)), const((rows_n, SB_W)), const((PAGE, PAGE + LANES)),
        ] + [page_spec(pidx) for pidx in range(ppb)] * 2,
        out_specs=pl.BlockSpec((1, dec, SB_W), lambda b, g, pt: (b, 0, 0)),
        scratch_shapes=[
            pltpu.VMEM((rows_n, SB_W), BF16),
            pltpu.VMEM((rows_n, SB_W), F32),
            pltpu.VMEM((rows_n, PAGE), F32),
        ],
    )
    o = pl.pallas_call(
        functools.partial(_sb_sample_kernel, n_pages=n_pages, ppb=ppb, dec=dec),
        grid_spec=grid_spec,
        out_shape=jax.ShapeDtypeStruct((batch, dec, SB_W), BF16),
        compiler_params=_cparams(("arbitrary", "arbitrary")),
    )(page_table, q3, as_page(k_rows), as_page(v_rows), bias_rows, bd, _cum_matrix(PAGE),
      *([cache_k] * ppb), *([cache_v] * ppb))
    return o.reshape(n, SB_W)


def _outproj_kernel(og_ref, or_ref, os_ref, x_ref, g1_ref, sh_ref, sc_ref, ng_ref, wg_ref, wr_ref, ws_ref,
                    *rest, with_router):
    if with_router:
        router_ref, x1_ref, h2_ref, gates_ref = rest
    else:
        x1_ref, h2_ref = rest
    mix = _dot(og_ref[...], wg_ref[...]) + _dot(or_ref[...], wr_ref[...]) + _dot(os_ref[...], ws_ref[...])
    x1 = x_ref[...] + g1_ref[0] * mix
    x1_ref[...] = x1
    ms = jnp.mean(x1 * x1, axis=-1, keepdims=True)
    h2 = x1 * lax.rsqrt(ms + EPS) * ng_ref[...]
    h2 = h2 * (1.0 + sc_ref[0]) + sh_ref[0]
    h2_ref[...] = h2.astype(BF16)
    if with_router:
        n_exp = 8
        logits = _dot_hi(h2, router_ref[...])
        lane = lax.broadcasted_iota(jnp.int32, logits.shape, 1).astype(F32)
        neg = jnp.float32(-jnp.inf)
        lg = jnp.where(lane < n_exp, logits, neg)
        m1 = jnp.max(lg, axis=1, keepdims=True)
        i1 = jnp.min(jnp.where(lg == m1, lane, float(LANES)), axis=1, keepdims=True)
        lg2 = jnp.where(lane == i1, neg, lg)
        m2 = jnp.max(lg2, axis=1, keepdims=True)
        i2 = jnp.min(jnp.where(lg2 == m2, lane, float(LANES)), axis=1, keepdims=True)
        e2 = jnp.exp(m2 - m1)
        gates_ref[...] = jnp.where(lane == i1, 1.0 / (1.0 + e2), 0.0) + jnp.where(lane == i2, e2 / (1.0 + e2), 0.0)


def _outproj(o_gla, o_rwkv, o_sb, x, mod, norm_g, wg, wr, ws, router):
    n, d = x.shape
    nb = mod.shape[0]
    tm = _tile(n // nb, 512)
    tpb = (n // nb) // tm
    with_router = router is not None
    row = lambda w: pl.BlockSpec((tm, w), lambda i: (i, 0))
    const = lambda shape: pl.BlockSpec(shape, lambda i: (0,) * len(shape))
    in_specs = [row(GLA_V), row(RWKV_W), row(SB_W), row(d),
                _mod_spec(mod, tm, tpb, 2), _mod_spec(mod, tm, tpb, 3), _mod_spec(mod, tm, tpb, 4),
                const((1, d)), const((GLA_V, d)), const((RWKV_W, d)), const((SB_W, d))]
    args = [o_gla, o_rwkv, o_sb, x, mod, mod, mod, norm_g.reshape(1, d), wg, wr, ws]
    out_specs = [row(d), row(d)]
    out_shape = [jax.ShapeDtypeStruct((n, d), F32), jax.ShapeDtypeStruct((n, d), BF16)]
    if with_router:
        in_specs.append(const((d, LANES)))
        args.append(router)
        out_specs.append(row(LANES))
        out_shape.append(jax.ShapeDtypeStruct((n, LANES), F32))
    return pl.pallas_call(
        functools.partial(_outproj_kernel, with_router=with_router),
        grid=(n // tm,),
        in_specs=in_specs, out_specs=out_specs, out_shape=out_shape,
        compiler_params=_cparams(("arbitrary",)),
    )(*args)


def _ffn_kernel(h_ref, x1_ref, g2_ref, *rest, gated):
    if gated:
        gates_ref, w1_ref, w3_ref, w2_ref, o_ref, acc_scr = rest
    else:
        w1_ref, w3_ref, w2_ref, o_ref, acc_scr = rest
    e = pl.program_id(1)
    f = pl.program_id(2)

    @pl.when((e == 0) & (f == 0))
    def _():
        acc_scr[...] = jnp.zeros_like(acc_scr)

    h = h_ref[...]
    a = _dot(h, w1_ref[0])
    t = a * _sigmoid(a) * _dot(h, w3_ref[0])
    if gated:
        gates = gates_ref[...]
        lane = lax.broadcasted_iota(jnp.int32, gates.shape, 1)
        t = t * jnp.sum(jnp.where(lane == e, gates, 0.0), axis=1, keepdims=True)
    acc_scr[...] += _dot(t.astype(BF16), w2_ref[0])

    @pl.when((e == pl.num_programs(1) - 1) & (f == pl.num_programs(2) - 1))
    def _():
        o_ref[...] = x1_ref[...] + g2_ref[0] * acc_scr[...]


def _ffn(h2, x1, mod, w1, w3, w2, gates, tf):
    n, d = x1.shape
    n_exp, _, ff = w1.shape
    nb = mod.shape[0]
    tm = _tile(n // nb, 512)
    tpb = (n // nb) // tm
    gated = gates is not None
    row = lambda w: pl.BlockSpec((tm, w), lambda i, e, f: (i, 0))
    in_specs = [row(d), row(d), _mod_spec(mod, tm, tpb, 5)]
    args = [h2, x1, mod]
    if gated:
        in_specs.append(row(LANES))
        args.append(gates)
    in_specs += [
        pl.BlockSpec((1, d, tf), lambda i, e, f: (e, 0, f)),
        pl.BlockSpec((1, d, tf), lambda i, e, f: (e, 0, f)),
        pl.BlockSpec((1, tf, d), lambda i, e, f: (e, f, 0)),
    ]
    args += [w1, w3, w2]
    return pl.pallas_call(
        functools.partial(_ffn_kernel, gated=gated),
        grid=(n // tm, n_exp, ff // tf),
        in_specs=in_specs,
        out_specs=row(d),
        out_shape=jax.ShapeDtypeStruct((n, d), F32),
        scratch_shapes=[pltpu.VMEM((tm, d), F32)],
        compiler_params=_cparams(("arbitrary", "arbitrary", "arbitrary")),
    )(*args)


def _moe_kernel(cnt_ref, h_ref, x1_ref, g2_ref, gates_ref, pos_ref, w1_ref, w3_ref, w2_ref, o_ref,
                hc_scr, accc_scr, *, tm, ch):
    i = pl.program_id(0)
    e = pl.program_id(1)
    f = pl.program_id(2)
    last_f = f == pl.num_programs(2) - 1
    count = cnt_ref[i, e]
    rank = lax.broadcasted_iota(jnp.int32, (ch, 1), 0)

    def one_hot(c):
        return jnp.where(pos_ref[0, pl.ds(e, 1), :] == rank + c * ch, 1.0, 0.0).astype(BF16)

    @pl.when((e == 0) & (f == 0))
    def _():
        o_ref[...] = jnp.zeros_like(o_ref)

    for c in range(-(-tm // ch)):
        rows = pl.ds(c * ch, ch)

        @pl.when(c * ch < count)
        def _():
            @pl.when(f == 0)
            def _():
                hc_scr[rows, :] = _dot(one_hot(c), h_ref[...]).astype(BF16)
                accc_scr[rows, :] = jnp.zeros((ch, accc_scr.shape[1]), F32)

            hc = hc_scr[rows, :]
            a = _dot(hc, w1_ref[0, 0])
            t = a * _sigmoid(a) * _dot(hc, w3_ref[0, 0])
            accc_scr[rows, :] += _dot(t.astype(BF16), w2_ref[0, 0])

            @pl.when(last_f)
            def _():
                y = accc_scr[rows, :]
                hi = y.astype(BF16)
                lo = (y - hi.astype(F32)).astype(BF16)
                p = one_hot(c)
                gates = gates_ref[...]
                lane = lax.broadcasted_iota(jnp.int32, gates.shape, 1)
                ge = jnp.sum(jnp.where(lane == e, gates, 0.0), axis=1, keepdims=True)
                o_ref[...] += ge * (_dot_tn(p, hi) + _dot_tn(p, lo))

    @pl.when((e == pl.num_programs(1) - 1) & last_f)
    def _():
        o_ref[...] = x1_ref[...] + g2_ref[0] * o_ref[...]


def _moe(h2, x1, mod, w1, w3, w2, gates, tf, layer):
    n, d = x1.shape
    _, n_exp, _, ff = w1.shape
    nb = mod.shape[0]
    tm = _tile(n // nb, 1024)
    tpb = (n // nb) // tm
    ch = -(-int(tm * TOP_K / n_exp * 1.125) // 32) * 32 if tm >= 512 else tm // 2
    slots = -(-tm // ch)
    n_tiles = n // tm
    routed = (gates[:, :n_exp] > 0.0).reshape(n_tiles, tm, n_exp)
    counts = jnp.sum(routed, axis=1).astype(jnp.int32)
    ranks = jnp.cumsum(routed.astype(jnp.int32), axis=1) - 1
    pos = jnp.swapaxes(jnp.where(routed, ranks, -1), 1, 2)
    row = lambda w: pl.BlockSpec((tm, w), lambda i, e, f, cnt: (i, 0))
    grid_spec = pltpu.PrefetchScalarGridSpec(
        num_scalar_prefetch=1,
        grid=(n_tiles, n_exp, ff // tf),
        in_specs=[
            row(d), row(d), _mod_spec(mod, tm, tpb, 5), row(LANES),
            pl.BlockSpec((1, n_exp, tm), lambda i, e, f, cnt: (i, 0, 0)),
            pl.BlockSpec((1, 1, d, tf), lambda i, e, f, cnt: (layer, e, 0, f)),
            pl.BlockSpec((1, 1, d, tf), lambda i, e, f, cnt: (layer, e, 0, f)),
            pl.BlockSpec((1, 1, tf, d), lambda i, e, f, cnt: (layer, e, f, 0)),
        ],
        out_specs=row(d),
        scratch_shapes=[pltpu.VMEM((slots * ch, d), BF16), pltpu.VMEM((slots * ch, d), F32)],
    )
    return pl.pallas_call(
        functools.partial(_moe_kernel, tm=tm, ch=ch),
        grid_spec=grid_spec,
        out_shape=jax.ShapeDtypeStruct((n, d), F32),
        compiler_params=_cparams(("arbitrary", "arbitrary", "arbitrary")),
    )(counts, h2, x1, mod, gates, pos, w1, w3, w2)


def _pad_rows(w, rows, offset):
    return jnp.pad(w, ((0, 0), (offset, rows - offset - w.shape[1]), (0, 0)))


def _layer(l, x, mod, batch, state, W, sb_fn, sb_bias=None):
    n = x.shape[0]
    gla0, rwkv0, shift0 = state
    p_gla, p_rwkv, p_sb = _inproj(x, mod, W['norm1_g'][l], W['w_in'][l])
    chunk = math.gcd(n // batch, GLA_CHUNK)
    o_gla, gla_st = _gla(p_gla, gla0, W['gla_wa2'][l], W['gla_ba'][l], W['gla_ng'][l], batch, chunk)
    o_rwkv, rwkv_st = _rwkv(p_rwkv, shift0, rwkv0, {k: v[l] for k, v in W['rwkv'].items()}, batch, 2)
    qs, k_rows, kb, vb = _sbprep(p_sb, W['sb_qg'][l], W['sb_kg'][l], sb_bias)
    v_rows = p_sb[:, 2 * SB_W:]
    o_sb = sb_fn(qs, kb, vb, k_rows, v_rows)
    router = W['router'][l // 2] if l % 2 else None
    outs = _outproj(o_gla, o_rwkv, o_sb, x, mod, W['norm2_g'][l], W['wo_g'][l], W['wo_r'][l], W['wo_s'][l], router)
    if l % 2 == 0:
        x1, h2 = outs
        x = _ffn(h2, x1, mod, W['ffn_w1'][l // 2][None], W['ffn_w3'][l // 2][None], W['ffn_w2'][l // 2][None],
                 None, W['ffn_w1'].shape[-1] // 2)
    else:
        x1, h2, gates = outs
        x = _moe(h2, x1, mod, W['moe_w1'], W['moe_w3'], W['moe_w2'], gates, W['moe_w1'].shape[-1] // 7, l // 2)
    t = n // batch
    shift_row = p_rwkv.reshape(batch, t, RWKV_COLS)[:, -1]
    return x, (k_rows, v_rows, _gla_state_from_bd(gla_st), _rwkv_state_out(rwkv_st), shift_row)


def kernel(x_prompt, x_sample, cache_sb_k, cache_sb_v, state_gla, state_rwkv, state_shift, page_table, c_prompt, c_sample, norm1_g, norm2_g, w_ada, b_ada, w_in, w_out, gla_w_a2, gla_b_a, gla_norm_g, rwkv_mu, rwkv_w0, rwkv_w_w2, rwkv_a0, rwkv_w_a2, rwkv_w_g2, rwkv_k_k, rwkv_k_a, rwkv_r_k, rwkv_ln_g, rwkv_ln_b, sb_q_g, sb_k_g, sb_bias, ffn_w1, ffn_w3, ffn_w2, moe_router, moe_w1, moe_w3, moe_w2):
    depth = w_in.shape[0]
    bp, seq, d = x_prompt.shape
    bs, dec, _ = x_sample.shape
    n_pool, page = cache_sb_k.shape[1:3]

    g0 = 0
    gq, gk, gv = w_in[:, :, g0:g0 + GLA_QK], w_in[:, :, g0 + GLA_QK:g0 + 2 * GLA_QK], w_in[:, :, g0 + 2 * GLA_QK:g0 + 2 * GLA_QK + GLA_V]
    g_low = w_in[:, :, g0 + 2 * GLA_QK + GLA_V:g0 + 2 * GLA_QK + GLA_V + GLA_LORA]
    g_r = w_in[:, :, g0 + 2 * GLA_QK + GLA_V + GLA_LORA:g0 + 2 * GLA_QK + 2 * GLA_V + GLA_LORA]
    gla_cols = 2 * GLA_QK + 2 * GLA_V + GLA_LORA
    g_low = jnp.pad(g_low, ((0, 0), (0, 0), (0, LANES - GLA_LORA)))
    w_in_r = jnp.concatenate([gq, gk, gv, g_r, g_low, w_in[:, :, gla_cols:]], axis=-1).astype(BF16)

    tile_heads = lambda g, h: jnp.tile(g, (1, h))
    W = dict(
        norm1_g=norm1_g, norm2_g=norm2_g, w_in=w_in_r,
        wo_g=w_out[:, :GLA_V].astype(BF16), wo_r=w_out[:, GLA_V:GLA_V + RWKV_W].astype(BF16),
        wo_s=w_out[:, GLA_V + RWKV_W:].astype(BF16),
        gla_wa2=_pad_rows(gla_w_a2, LANES, 0), gla_ba=gla_b_a[:, None, :],
        gla_ng=tile_heads(gla_norm_g, GLA_HEADS)[:, None, :],
        rwkv=dict(
            mu=rwkv_mu[:, None, :], w0=rwkv_w0[:, None, :], ww2=_pad_rows(rwkv_w_w2, LANES, 0),
            a0=rwkv_a0[:, None, :], wa2=_pad_rows(rwkv_w_a2, LANES, RWKV_W_LORA), wg2=rwkv_w_g2,
            kk=rwkv_k_k[:, None, :], ka=rwkv_k_a[:, None, :], rk=rwkv_r_k.reshape(depth, 1, RWKV_W),
            lng=rwkv_ln_g[:, None, :], lnb=rwkv_ln_b[:, None, :]),
        sb_qg=tile_heads(sb_q_g, SB_HEADS)[:, None, :], sb_kg=tile_heads(sb_k_g, SB_HEADS)[:, None, :],
        ffn_w1=ffn_w1.astype(BF16), ffn_w3=ffn_w3.astype(BF16), ffn_w2=ffn_w2.astype(BF16),
        router=jnp.pad(moe_router, ((0, 0), (0, 0), (0, LANES - moe_router.shape[-1]))),
        moe_w1=moe_w1.astype(BF16), moe_w3=moe_w3.astype(BF16), moe_w2=moe_w2.astype(BF16),
    )

    rows = bp + bs
    rows_pad = -(-rows // 8) * 8
    c_all = jnp.pad(jnp.concatenate([c_prompt, c_sample], axis=0), ((0, rows_pad - rows), (0, 0)))
    mod_all = _modulation(c_all, w_ada, b_ada)

    xp = x_prompt.reshape(bp * seq, d)
    xs = x_sample.reshape(bs * dec, d)
    cache_k = jnp.transpose(cache_sb_k, (0, 1, 3, 4, 2))
    cache_v = jnp.transpose(cache_sb_v, (0, 1, 3, 4, 2))
    zero_p = (jnp.zeros((bp, GLA_V, GLA_QK), F32), jnp.zeros((bp, HEAD_DIM, RWKV_W), F32),
              jnp.zeros((bp, 1, RWKV_COLS), F32))
    st_p, st_s = [], []
    for l in range(depth):
        mod_p = mod_all[l, :bp][:, None, :]
        mod_s = jnp.repeat(mod_all[l, bp:rows], dec, axis=0)[None]
        xp, sp = _layer(l, xp, mod_p, bp, zero_p, W,
                        lambda q, kb, vb, k_rows, v_rows: _sb_prompt(q, kb, vb, bp), sb_bias[l])
        state_s = (_gla_state_to_bd(state_gla[l]), _rwkv_state_in(state_rwkv[l]), state_shift[l][:, None, :])
        xs, ss = _layer(l, xs, mod_s, bs, state_s, W,
                        lambda q, kb, vb, k_rows, v_rows: _sb_sample(q, k_rows, v_rows, cache_k, cache_v, l,
                                                                     page_table, sb_bias[l], bs))
        st_p.append(sp)
        st_s.append(ss)

    def stack(sts, idx, shape):
        return jnp.stack([s[idx] for s in sts]).reshape((depth,) + shape)

    outs = [xp.reshape(bp, seq, d), xs.reshape(bs, dec, d)]
    for sts, b, t in ((st_p, bp, seq), (st_s, bs, dec)):
        outs += [
            stack(sts, 0, (b, t, SB_HEADS, HEAD_DIM)), stack(sts, 1, (b, t, SB_HEADS, HEAD_DIM)),
            stack(sts, 2, (b, GLA_HEADS, GLA_DK, GLA_DV)), stack(sts, 3, (b, RWKV_HEADS, HEAD_DIM, HEAD_DIM)),
            stack(sts, 4, (b, RWKV_COLS)),
        ]
    return tuple(outs)
```

```python
import functools
import math

import numpy as np
import jax
import jax.numpy as jnp
from jax import lax
from jax.experimental import pallas as pl
from jax.experimental.pallas import tpu as pltpu

F32 = jnp.float32
BF16 = jnp.bfloat16
HIGHEST = lax.Precision.HIGHEST

HEAD_DIM = 64
GLA_HEADS = 4
GLA_DK = 32
GLA_DV = 64
GLA_LORA = 16
GLA_TAU = 16.0
GLA_CHUNK = 16
RWKV_HEADS = 4
RWKV_W_LORA = 64
RWKV_A_LORA = 64
RWKV_G_LORA = 128
RWKV_LN_EPS = 64e-5
SB_HEADS = 8
TOP_K = 2
EPS = 1e-6
LOG2E = math.log2(math.e)
SB_DEAD_BITS = 160.0

GLA_QK = GLA_HEADS * GLA_DK
GLA_V = GLA_HEADS * GLA_DV
GLA_PCOLS = 2 * GLA_QK + 2 * GLA_V + 128
RWKV_W = RWKV_HEADS * HEAD_DIM
RWKV_COLS = 3 * RWKV_W + RWKV_W_LORA + RWKV_A_LORA + RWKV_G_LORA
SB_W = SB_HEADS * HEAD_DIM
SB_COLS = 3 * SB_W
LANES = 128
PAGE = 128
VMEM_LIMIT = 56 * 1024 * 1024


def _cparams(sem):
    return pltpu.CompilerParams(dimension_semantics=sem, vmem_limit_bytes=VMEM_LIMIT)


def _tile(n, pref):
    if n <= pref:
        return n
    t = pref
    while n % t:
        t -= 8
    return t


def _block_ones(n, group, value=1.0):
    idx = np.arange(n) // group
    return jnp.asarray((idx[:, None] == idx[None, :]).astype(np.float32) * value)


def _dot(a, b):
    return jnp.dot(a, b, preferred_element_type=F32)


def _dot_hi(a, b):
    return jnp.dot(a, b, preferred_element_type=F32, precision=HIGHEST)


def _dot_nt(a, b, precision=None):
    return lax.dot_general(a, b, (((1,), (1,)), ((), ())), preferred_element_type=F32, precision=precision)


def _dot_tn(a, b, precision=None):
    return lax.dot_general(a, b, (((0,), (0,)), ((), ())), preferred_element_type=F32, precision=precision)


def _dot_split(x, m_bf16):
    hi = x.astype(BF16)
    lo = (x - hi.astype(F32)).astype(BF16)
    return _dot(hi, m_bf16) + _dot(lo, m_bf16)


def _neg_abs(x):
    bits = lax.bitcast_convert_type(x, jnp.uint32) | jnp.uint32(0x80000000)
    return lax.bitcast_convert_type(bits, F32)


def _sigmoid(x):
    return 1.0 / (1.0 + jnp.exp(-x))


def _softplus(x):
    return jnp.maximum(x, 0.0) + jnp.log(1.0 + jnp.exp(-jnp.abs(x)))


def _mod_kernel(c_ref, w_ref, b_ref, o_ref):
    c = c_ref[...]
    o_ref[0] = _dot_hi(c * _sigmoid(c), w_ref[0]) + b_ref[0]


def _modulation(c_all, w_ada, b_ada):
    depth, d, d6 = w_ada.shape
    r = c_all.shape[0]
    tn = 1024
    return pl.pallas_call(
        _mod_kernel,
        grid=(depth, d6 // tn),
        in_specs=[
            pl.BlockSpec((r, d), lambda l, j: (0, 0)),
            pl.BlockSpec((1, d, tn), lambda l, j: (l, 0, j)),
            pl.BlockSpec((1, 1, tn), lambda l, j: (l, 0, j)),
        ],
        out_specs=pl.BlockSpec((1, r, tn), lambda l, j: (l, 0, j)),
        out_shape=jax.ShapeDtypeStruct((depth, r, d6), F32),
        compiler_params=_cparams(("arbitrary", "arbitrary")),
    )(c_all, w_ada, b_ada.reshape(depth, 1, d6))


def _mod_spec(mod, tm, tpb, chunk):
    d = mod.shape[2] // 6
    if mod.shape[1] == 1:
        return pl.BlockSpec((1, 1, d), lambda i, *_: (i // tpb, 0, chunk))
    return pl.BlockSpec((1, tm, d), lambda i, *_: (i // tpb, i % tpb, chunk))


def _inproj_kernel(x_ref, sh_ref, sc_ref, g_ref, w_ref, pg_ref, pr_ref, ps_ref):
    x = x_ref[...]
    ms = jnp.mean(x * x, axis=-1, keepdims=True)
    h = x * lax.rsqrt(ms + EPS) * g_ref[...]
    h = h * (1.0 + sc_ref[0]) + sh_ref[0]
    p = _dot(h.astype(BF16), w_ref[...])
    pg_ref[...] = p[:, :GLA_PCOLS]
    pr_ref[...] = p[:, GLA_PCOLS:GLA_PCOLS + RWKV_COLS]
    ps_ref[...] = p[:, GLA_PCOLS + RWKV_COLS:]


def _inproj(x, mod, norm_g, w):
    n, d = x.shape
    nb = mod.shape[0]
    tm = _tile(n // nb, 256)
    tpb = (n // nb) // tm
    ncols = w.shape[1]
    return pl.pallas_call(
        _inproj_kernel,
        grid=(n // tm,),
        in_specs=[
            pl.BlockSpec((tm, d), lambda i: (i, 0)),
            _mod_spec(mod, tm, tpb, 0),
            _mod_spec(mod, tm, tpb, 1),
            pl.BlockSpec((1, d), lambda i: (0, 0)),
            pl.BlockSpec((d, ncols), lambda i: (0, 0)),
        ],
        out_specs=[
            pl.BlockSpec((tm, GLA_PCOLS), lambda i: (i, 0)),
            pl.BlockSpec((tm, RWKV_COLS), lambda i: (i, 0)),
            pl.BlockSpec((tm, SB_COLS), lambda i: (i, 0)),
        ],
        out_shape=[
            jax.ShapeDtypeStruct((n, GLA_PCOLS), F32),
            jax.ShapeDtypeStruct((n, RWKV_COLS), F32),
            jax.ShapeDtypeStruct((n, SB_COLS), F32),
        ],
        compiler_params=_cparams(("arbitrary",)),
    )(x, mod, mod, norm_g.reshape(1, d), w)


def _gla_kernel(p_ref, s0_ref, wa2_ref, ba_ref, ng_ref, hexp_ref, havg_ref, bd_ref, tril_ref, ones_ref,
                o_ref, s_ref, b_scr, qin_scr, kout_scr, dec_scr, o_scr, *, chunk, n_chunks):
    c_len = chunk

    @pl.when(pl.program_id(1) == 0)
    def _():
        s_ref[0] = s0_ref[0]

    glow = p_ref[0, :, 2 * GLA_QK + 2 * GLA_V:]
    x = _dot_hi(glow, wa2_ref[...]) + ba_ref[...]
    la = -_softplus(-x) * (1.0 / GLA_TAU)
    la_hi = la.astype(BF16)
    la_lo = (la - la_hi.astype(F32)).astype(BF16)
    b = _dot(tril_ref[...], la_hi) + _dot(tril_ref[...], la_lo)
    b_tot = _dot(ones_ref[...], la_hi) + _dot(ones_ref[...], la_lo)
    b_scr[...] = b
    qin_scr[...] = p_ref[0, :, 0:GLA_QK] * (GLA_DK ** -0.5) * jnp.exp(b)
    kout_scr[...] = p_ref[0, :, GLA_QK:2 * GLA_QK] * jnp.exp(b_tot - b)
    dec_scr[...] = jnp.exp(b_tot)
    row = lax.broadcasted_iota(jnp.int32, (c_len, 1), 0)
    hexp = hexp_ref[...]
    bd = bd_ref[...]

    def body(c, carry):
        sl = pl.ds(pl.multiple_of(c * c_len, c_len), c_len)
        q = p_ref[0, sl, 0:GLA_QK] * (GLA_DK ** -0.5)
        k = p_ref[0, sl, GLA_QK:2 * GLA_QK]
        v = p_ref[0, sl, 2 * GLA_QK:2 * GLA_QK + GLA_V]
        b = b_scr[sl, :]
        zs = []
        for s in range(c_len):
            m = row >= s
            e = jnp.exp(jnp.where(m, b - b[s:s + 1, :], 0.0))
            zs.append(jnp.where(m, q * k[s:s + 1, :] * e, 0.0))
        a_exp = _dot_split(jnp.concatenate(zs, axis=0), hexp)
        o = jnp.zeros((c_len, GLA_V), F32)
        for s in range(c_len):
            o = o + a_exp[s * c_len:(s + 1) * c_len, :] * v[s:s + 1, :]
        st = s_ref[0]
        o_scr[sl, :] = o + _dot_nt(qin_scr[sl, :], st, HIGHEST)
        s_ref[0] = st * dec_scr[pl.ds(c * c_len, 1), :] + _dot_tn(v, kout_scr[sl, :], HIGHEST) * bd
        return carry

    lax.fori_loop(0, n_chunks, body, 0, unroll=min(4, n_chunks))
    o = o_scr[...]
    r = p_ref[0, :, 2 * GLA_QK + GLA_V:2 * GLA_QK + 2 * GLA_V]
    on = o * lax.rsqrt(_dot_split(o * o, havg_ref[...]) + EPS) * ng_ref[...]
    o_ref[0] = (on * (r * _sigmoid(r))).astype(o_ref.dtype)


def _gla(p_gla, s0t, wa2p, ba, ng_t, batch, chunk):
    n = p_gla.shape[0]
    t = n // batch
    tt = _tile(t, 512)
    n_chunks = tt // chunk
    p3 = p_gla.reshape(batch, t, GLA_PCOLS)
    hexp = jnp.asarray((np.arange(GLA_QK)[:, None] // GLA_DK == np.arange(GLA_V)[None, :] // GLA_DV).astype(np.float32))
    havg = _block_ones(GLA_V, GLA_DV, 1.0 / GLA_DV)
    bd = jnp.asarray((np.arange(GLA_V)[:, None] // GLA_DV == np.arange(GLA_QK)[None, :] // GLA_DK).astype(np.float32))
    rows = np.arange(tt)
    same = rows[:, None] // chunk == rows[None, :] // chunk
    tril = jnp.asarray((same & (rows[:, None] >= rows[None, :])).astype(np.float32)).astype(BF16)
    ones = jnp.asarray(same.astype(np.float32)).astype(BF16)
    const = lambda shape: pl.BlockSpec(shape, lambda b, i: (0,) * len(shape))
    scr = lambda w: pltpu.VMEM((tt, w), F32)
    o, s = pl.pallas_call(
        functools.partial(_gla_kernel, chunk=chunk, n_chunks=n_chunks),
        grid=(batch, t // tt),
        in_specs=[
            pl.BlockSpec((1, tt, GLA_PCOLS), lambda b, i: (b, i, 0)),
            pl.BlockSpec((1, GLA_V, GLA_QK), lambda b, i: (b, 0, 0)),
            const((LANES, GLA_QK)), const((1, GLA_QK)), const((1, GLA_V)),
            const((GLA_QK, GLA_V)), const((GLA_V, GLA_V)), const((GLA_V, GLA_QK)), const((tt, tt)), const((tt, tt)),
        ],
        out_specs=[
            pl.BlockSpec((1, tt, GLA_V), lambda b, i: (b, i, 0)),
            pl.BlockSpec((1, GLA_V, GLA_QK), lambda b, i: (b, 0, 0)),
        ],
        out_shape=[
            jax.ShapeDtypeStruct((batch, t, GLA_V), BF16),
            jax.ShapeDtypeStruct((batch, GLA_V, GLA_QK), F32),
        ],
        scratch_shapes=[scr(GLA_QK), scr(GLA_QK), scr(GLA_QK), scr(GLA_QK), scr(GLA_V)],
        compiler_params=_cparams(("arbitrary", "arbitrary")),
    )(p3, s0t, wa2p, ba, ng_t, hexp.astype(BF16), havg.astype(BF16), bd, tril, ones)
    return o.reshape(n, GLA_V), s


def _gla_state_to_bd(s0):
    b = s0.shape[0]
    eye = jnp.eye(GLA_HEADS, dtype=s0.dtype)
    st = jnp.einsum('bhkv,hg->bhvgk', s0, eye)
    return st.reshape(b, GLA_V, GLA_QK)


def _gla_state_from_bd(st):
    b = st.shape[0]
    s5 = st.reshape(b, GLA_HEADS, GLA_DV, GLA_HEADS, GLA_DK)
    diag = jnp.stack([s5[:, h, :, h, :] for h in range(GLA_HEADS)], axis=1)
    return jnp.swapaxes(diag, 2, 3)


def _rwkv_kernel(p_ref, sh0_ref, s0_ref, mu_ref, w0_ref, ww2_ref, a0_ref, wa2_ref, wg2_ref, kk_ref, ka_ref,
                 rk_ref, lng_ref, lnb_ref, ones_ref, onesb_ref, avg_ref, sel_ref,
                 o_ref, s_ref, prev_scr, nkk_scr, d_scr, b_scr, k_scr, r_scr, v_scr, y_scr, *, nbb, tt):
    first = pl.program_id(1) == 0

    @pl.when(first)
    def _():
        s_ref[...] = s0_ref[...]
        prev_scr[...] = sh0_ref[...]

    ones_f = ones_ref[...]
    ones_b = onesb_ref[...]
    avg = avg_ref[...]
    sel = sel_ref[...]
    row = lax.broadcasted_iota(jnp.int32, (tt, 1), 0)
    gate = []
    for bb in range(nbb):
        p = p_ref[bb]
        p_prev = jnp.where(row == 0, prev_scr[bb], pltpu.roll(p, 1, axis=0))
        prev_scr[bb] = p[tt - 1:tt, :]
        xm = p + (p_prev - p) * mu_ref[...]
        r = xm[:, 0:RWKV_W]
        k = xm[:, RWKV_W:2 * RWKV_W]
        v = xm[:, 2 * RWKV_W:3 * RWKV_W]
        xwa = xm[:, 3 * RWKV_W:3 * RWKV_W + LANES]
        xg = xm[:, 3 * RWKV_W + LANES:]
        w = -_softplus(-(w0_ref[...] + _dot_hi(jnp.tanh(xwa), ww2_ref[...]))) - 0.5
        a = _sigmoid(a0_ref[...] + _dot_hi(xwa, wa2_ref[...]))
        gate.append(_dot_hi(_sigmoid(xg), wg2_ref[...]))
        kk = k * kk_ref[...]
        nrm = jnp.sqrt(_dot_hi(kk * kk, ones_f))
        kk = kk / jnp.maximum(nrm, 1e-12)
        k2 = k * (1.0 + (a - 1.0) * ka_ref[...])
        nkk_scr[bb] = -kk
        d_scr[bb] = jnp.exp(-jnp.exp(w))
        b_scr[bb] = kk * a
        k_scr[bb] = k2
        r_scr[bb] = r
        v_scr[bb] = v

    def head_sums(xs, split):
        stacked = jnp.concatenate([x[:, half * LANES:(half + 1) * LANES] for x in xs for half in range(2)], axis=0)
        res = _dot_split(stacked, ones_b) if split else _dot(stacked.astype(BF16), ones_b)
        out, row = [], 0
        for x in xs:
            n_rows = x.shape[0]
            out.append(jnp.concatenate([res[row:row + n_rows], res[row + n_rows:row + 2 * n_rows]], axis=1))
            row += 2 * n_rows
        return out

    lane_lo = lax.broadcasted_iota(jnp.int32, (1, LANES), 1) < HEAD_DIM

    def seg_sums(x):
        halves = []
        for half in range(2):
            xh = x[:, half * LANES:(half + 1) * LANES]
            lo = jnp.sum(jnp.where(lane_lo, xh, 0.0), axis=1, keepdims=True)
            hi = jnp.sum(jnp.where(lane_lo, 0.0, xh), axis=1, keepdims=True)
            halves.append(jnp.where(lane_lo, lo, hi))
        return jnp.concatenate(halves, axis=1)

    def read_out(s, bb, ts):
        yb, = head_sums([s * jnp.broadcast_to(r_scr[bb, ts, :], (HEAD_DIM, RWKV_W))], False)
        y_scr[bb, ts, :] = jnp.sum(yb * sel, axis=0, keepdims=True)

    def step(t, states):
        ts = pl.ds(t, 1)
        tp = pl.ds(jnp.maximum(t - 1, 0), 1)
        bc = lambda ref, bb, rows: jnp.broadcast_to(ref[bb, rows, :], (HEAD_DIM, RWKV_W))
        sas = [seg_sums(states[bb] * bc(nkk_scr, bb, ts)) for bb in range(nbb)]
        side = head_sums([jnp.concatenate([states[bb] * bc(r_scr, bb, tp), sel * bc(v_scr, bb, ts)], axis=0)
                          for bb in range(nbb)], False)
        new = []
        for bb in range(nbb):
            y_scr[bb, tp, :] = jnp.sum(side[bb][0:HEAD_DIM] * sel, axis=0, keepdims=True)
            vb = side[bb][HEAD_DIM:]
            new.append(states[bb] * bc(d_scr, bb, ts) + sas[bb] * bc(b_scr, bb, ts) + vb * bc(k_scr, bb, ts))
        return tuple(new)

    states = lax.fori_loop(0, tt, step, tuple(s_ref[bb] for bb in range(nbb)), unroll=2)
    for bb in range(nbb):
        s_ref[bb] = states[bb]
        read_out(states[bb], bb, pl.ds(tt - 1, 1))
        y = y_scr[bb]
        mean = _dot_hi(y, avg)
        yc = y - mean
        var = _dot_hi(yc * yc, avg)
        yn = yc * lax.rsqrt(var + RWKV_LN_EPS) * lng_ref[...] + lnb_ref[...]
        bonus = _dot_hi(r_scr[bb] * k_scr[bb] * rk_ref[...], ones_f) * v_scr[bb]
        o_ref[bb] = ((yn + bonus) * gate[bb]).astype(o_ref.dtype)


def _rwkv(p_rwkv, shift0, s0, wl, batch, nbb):
    n = p_rwkv.shape[0]
    t = n // batch
    tt = _tile(t, 256)
    p3 = p_rwkv.reshape(batch, t, RWKV_COLS)
    ones_f = _block_ones(RWKV_W, HEAD_DIM)
    avg = _block_ones(RWKV_W, HEAD_DIM, 1.0 / HEAD_DIM)
    sel = jnp.asarray((np.arange(HEAD_DIM)[:, None] == np.arange(RWKV_W)[None, :] % HEAD_DIM).astype(np.float32))
    const = lambda shape: pl.BlockSpec(shape, lambda b, i: (0,) * len(shape))
    row = const((1, RWKV_W))
    scr = lambda: pltpu.VMEM((nbb, tt, RWKV_W), F32)
    o, s = pl.pallas_call(
        functools.partial(_rwkv_kernel, nbb=nbb, tt=tt),
        grid=(batch // nbb, t // tt),
        in_specs=[
            pl.BlockSpec((nbb, tt, RWKV_COLS), lambda b, i: (b, i, 0)),
            pl.BlockSpec((nbb, 1, RWKV_COLS), lambda b, i: (b, 0, 0)),
            pl.BlockSpec((nbb, HEAD_DIM, RWKV_W), lambda b, i: (b, 0, 0)),
            const((1, RWKV_COLS)), row, const((LANES, RWKV_W)), row, const((LANES, RWKV_W)),
            const((RWKV_G_LORA, RWKV_W)), row, row, row, row, row,
            const((RWKV_W, RWKV_W)), const((LANES, LANES)), const((RWKV_W, RWKV_W)), const((HEAD_DIM, RWKV_W)),
        ],
        out_specs=[
            pl.BlockSpec((nbb, tt, RWKV_W), lambda b, i: (b, i, 0)),
            pl.BlockSpec((nbb, HEAD_DIM, RWKV_W), lambda b, i: (b, 0, 0)),
        ],
        out_shape=[
            jax.ShapeDtypeStruct((batch, t, RWKV_W), BF16),
            jax.ShapeDtypeStruct((batch, HEAD_DIM, RWKV_W), F32),
        ],
        scratch_shapes=[pltpu.VMEM((nbb, 1, RWKV_COLS), F32)] + [scr() for _ in range(7)],
        compiler_params=_cparams(("arbitrary", "arbitrary")),
    )(p3, shift0, s0, wl['mu'], wl['w0'], wl['ww2'], wl['a0'], wl['wa2'], wl['wg2'], wl['kk'], wl['ka'],
      wl['rk'], wl['lng'], wl['lnb'], ones_f, _block_ones(LANES, HEAD_DIM).astype(BF16), avg, sel)
    return o.reshape(n, RWKV_W), s


def _rwkv_state_in(s0):
    b = s0.shape[0]
    return jnp.transpose(s0, (0, 2, 1, 3)).reshape(b, HEAD_DIM, RWKV_W)


def _rwkv_state_out(s):
    b = s.shape[0]
    return jnp.transpose(s.reshape(b, HEAD_DIM, RWKV_HEADS, HEAD_DIM), (0, 2, 1, 3))


def _sbprep_kernel(p_ref, qg_ref, kg_ref, avg_ref, *rest, aug):
    if aug:
        qrow_ref, krow_ref, q_ref, k_ref, kb_ref, vb_ref = rest
    else:
        q_ref, k_ref, kb_ref, vb_ref = rest
    avg = avg_ref[...]
    q = p_ref[:, 0:SB_W]
    k = p_ref[:, SB_W:2 * SB_W]
    qn = q * lax.rsqrt(_dot_hi(q * q, avg) + EPS) * (qg_ref[...] * (HEAD_DIM ** -0.5 * LOG2E))
    kn = k * lax.rsqrt(_dot_hi(k * k, avg) + EPS) * kg_ref[...]
    k_ref[...] = kn
    vb_ref[...] = p_ref[:, 2 * SB_W:].astype(BF16)
    if aug:
        lo = lax.broadcasted_iota(jnp.int32, (1, LANES), 1) < HEAD_DIM

        def widen(x, row):
            parts = []
            for h in range(SB_HEADS):
                blk = x[:, (h // 2) * LANES:(h // 2 + 1) * LANES]
                if h % 2:
                    blk = pltpu.roll(blk, HEAD_DIM, axis=1)
                parts.append(jnp.where(lo, blk, 0.0))
            return (jnp.concatenate(parts, axis=1) + row).astype(BF16)

        q_ref[...] = widen(qn, qrow_ref[...])
        kb_ref[...] = widen(kn, krow_ref[...])
    else:
        q_ref[...] = qn.astype(BF16)
        kb_ref[...] = kn.astype(BF16)


def _sbprep(p_sb, qg_t, kg_t, bias=None):
    n = p_sb.shape[0]
    tm = _tile(n, 512)
    aug = bias is not None
    avg = _block_ones(SB_W, HEAD_DIM, 1.0 / HEAD_DIM)
    wide = SB_HEADS * LANES if aug else SB_W
    out = lambda w: pl.BlockSpec((tm, w), lambda i: (i, 0))
    const = lambda shape: pl.BlockSpec(shape, lambda i: (0,) * len(shape))
    in_specs = [pl.BlockSpec((tm, SB_COLS), lambda i: (i, 0)), const((1, SB_W)), const((1, SB_W)), const((SB_W, SB_W))]
    args = [p_sb, qg_t, kg_t, avg]
    if aug:
        b2 = bias * LOG2E
        hi = b2.astype(BF16).astype(F32)
        lo = (b2 - hi).astype(BF16).astype(F32)
        qrow = jnp.zeros((SB_HEADS, LANES), F32).at[:, HEAD_DIM].set(hi).at[:, HEAD_DIM + 1].set(lo)
        krow = jnp.zeros((SB_HEADS, LANES), F32).at[:, HEAD_DIM:HEAD_DIM + 2].set(1.0)
        in_specs += [const((1, wide)), const((1, wide))]
        args += [qrow.reshape(1, wide), krow.reshape(1, wide)]
    return pl.pallas_call(
        functools.partial(_sbprep_kernel, aug=aug),
        grid=(n // tm,),
        in_specs=in_specs,
        out_specs=[out(wide), out(SB_W), out(wide), out(SB_W)],
        out_shape=[
            jax.ShapeDtypeStruct((n, wide), BF16),
            jax.ShapeDtypeStruct((n, SB_W), F32),
            jax.ShapeDtypeStruct((n, wide), BF16),
            jax.ShapeDtypeStruct((n, SB_W), BF16),
        ],
        compiler_params=_cparams(("arbitrary",)),
    )(*args)


def _sb_keep(z, mask):
    keep = jnp.maximum(z, 0.0) + jnp.log2(1.0 + jnp.exp2(-jnp.abs(z)))
    if mask is not None:
        keep = jnp.where(mask, keep, 0.0)
    return z - keep, keep.astype(BF16)


def _sb_logits_stage(z, mask, cum):
    zk, kb = _sb_keep(z, mask)
    tk = z.shape[1]
    cs = _dot(kb, cum)
    return zk - cs[:, :tk], cs[:, tk:]


def _sb_weights(part, r_run, mask):
    reps = part.shape[1] // r_run.shape[1]
    e = jnp.exp2(part - (jnp.concatenate([r_run] * reps, axis=1) if reps > 1 else r_run))
    if mask is not None:
        e = jnp.where(mask, e, 0.0)
    return e


def _sb_prompt_kernel(q_ref, k_ref, v_ref, later_ref, o_ref, zk_scr, kb_scr, tot_scr, *, tq):
    i = pl.program_id(2)
    later = later_ref[...]
    rows = lax.broadcasted_iota(jnp.int32, (tq, tq), 0)
    cols = lax.broadcasted_iota(jnp.int32, (tq, tq), 1)
    diag = cols < rows

    def stage1(j, mask):
        ks = pl.ds(pl.multiple_of(j * tq, tq), tq)
        out = []
        for u in range(2):
            z = _dot_nt(q_ref[:, u * LANES:(u + 1) * LANES], k_ref[ks, u * LANES:(u + 1) * LANES])
            keep = jnp.maximum(z, 0.0) + jnp.log2(1.0 + jnp.exp2(_neg_abs(z)))
            if mask is not None:
                keep = jnp.where(mask, keep, 0.0)
            tot = jnp.broadcast_to(jnp.sum(keep, axis=1, keepdims=True), (tq, LANES))
            out.append((z - keep, keep.astype(BF16), tot))
        return out

    def stage2(j, pend, carry, mask, valid):
        vb = v_ref[pl.ds(pl.multiple_of(j * tq, tq), tq), :]
        out = []
        for u in range(2):
            zk, kb, tot = pend[u]
            acc, r_run = carry[u]
            pv = _dot(_sb_weights(zk - _dot(kb, later), r_run, mask).astype(BF16), vb)
            if valid is not None:
                pv = jnp.where(valid, pv, 0.0)
                tot = jnp.where(valid, tot, 0.0)
            out.append((acc + pv, r_run + tot))
        return tuple(out)

    carry = tuple((jnp.zeros((tq, LANES), F32), jnp.zeros((tq, LANES), F32)) for _ in range(2))
    carry = stage2(i, stage1(i, diag), carry, diag, None)
    zk_scr[...] = jnp.zeros_like(zk_scr)
    kb_scr[...] = jnp.zeros_like(kb_scr)
    tot_scr[...] = jnp.zeros_like(tot_scr)

    def half_trip(n, slot, carry):
        nxt = stage1(jnp.maximum(i - 1 - n, 0), None)
        pend = [(zk_scr[slot, u], kb_scr[slot, u], tot_scr[slot, u]) for u in range(2)]
        carry = stage2(jnp.maximum(i - n, 0), pend, carry, None, (n >= 1) & (n <= i))
        for u in range(2):
            zk_scr[1 - slot, u] = nxt[u][0]
            kb_scr[1 - slot, u] = nxt[u][1]
            tot_scr[1 - slot, u] = nxt[u][2]
        return carry

    def body(state):
        m, _, carry = state
        live = jnp.minimum(jnp.min(carry[0][1]), jnp.min(carry[1][1])) < SB_DEAD_BITS
        carry = half_trip(2 * m, 0, carry)
        carry = half_trip(2 * m + 1, 1, carry)
        return m + 1, live, carry

    _, _, carry = lax.while_loop(lambda s: (2 * s[0] < i + 1) & s[1], body, (jnp.int32(0), jnp.bool_(True), carry))
    lo = lax.broadcasted_iota(jnp.int32, (1, LANES), 1) < HEAD_DIM
    o_ref[...] = jnp.where(lo, carry[0][0], carry[1][0]).astype(o_ref.dtype)


def _cum_matrix(tk):
    j = np.arange(tk)
    later = (j[:, None] > j[None, :]).astype(np.float32)
    return jnp.asarray(np.concatenate([later, np.ones((tk, LANES), np.float32)], axis=1)).astype(BF16)


def _sb_prompt(q_aug, k_aug, vb, batch):
    n = vb.shape[0]
    t = n // batch
    tq = _tile(t, 256)
    tpb = t // tq
    return pl.pallas_call(
        functools.partial(_sb_prompt_kernel, tq=tq),
        grid=(batch, SB_HEADS // 2, tpb),
        in_specs=[
            pl.BlockSpec((tq, 2 * LANES), lambda b, h, i: (b * tpb + i, h)),
            pl.BlockSpec((t, 2 * LANES), lambda b, h, i: (b, h)),
            pl.BlockSpec((t, LANES), lambda b, h, i: (b, h)),
            pl.BlockSpec((tq, tq), lambda b, h, i: (0, 0)),
        ],
        out_specs=pl.BlockSpec((tq, LANES), lambda b, h, i: (b * tpb + i, h)),
        out_shape=jax.ShapeDtypeStruct((n, SB_W), BF16),
        scratch_shapes=[pltpu.VMEM((2, 2, tq, tq), F32), pltpu.VMEM((2, 2, tq, tq), BF16),
                        pltpu.VMEM((2, 2, tq, LANES), F32)],
        compiler_params=_cparams(("arbitrary", "arbitrary", "arbitrary")),
    )(q_aug, k_aug, vb, _cum_matrix(tq)[:, :tq])


def _sb_sample_kernel(pt_ref, q_ref, kn_ref, vn_ref, bias_ref, bd_ref, cum_ref, *rest, n_pages, ppb, dec):
    k_refs = rest[:ppb]
    v_refs = rest[ppb:2 * ppb]
    o_ref, qbd_scr, acc_scr, run_scr = rest[2 * ppb:]
    g = pl.program_id(1)
    rows_n = SB_HEADS * dec
    bd = bd_ref[...]
    cum = cum_ref[...]
    bias = bias_ref[...]

    def logits(kt, mask):
        return _sb_logits_stage(_dot(qbd_scr[...], kt.astype(BF16)) + bias, mask, cum)

    def attend(kt, vt, mask):
        part, tot = logits(kt, mask)
        acc_scr[...] += _dot_nt(_sb_weights(part, run_scr[...], mask).astype(BF16), vt.astype(BF16))
        run_scr[...] += tot

    @pl.when(g == 0)
    def _():
        q = q_ref[0].astype(F32)
        qbd_scr[...] = (jnp.concatenate([q] * SB_HEADS, axis=0) * bd).astype(BF16)
        acc_scr[...] = jnp.zeros_like(acc_scr)
        run_scr[...] = jnp.zeros_like(run_scr)
        qi = lax.broadcasted_iota(jnp.int32, (rows_n, PAGE), 0) % dec
        kj = lax.broadcasted_iota(jnp.int32, (rows_n, PAGE), 1)
        attend(kn_ref[0], vn_ref[0], kj < qi)

    pend = [logits(k_refs[pidx][0, 0].reshape(SB_W, PAGE), None) for pidx in range(ppb)]
    r_run = run_scr[...]
    acc = acc_scr[...]
    for pidx in range(ppb):
        part, tot = pend[pidx]
        vt = v_refs[pidx][0, 0].reshape(SB_W, PAGE).astype(BF16)
        acc = acc + _dot_nt(_sb_weights(part, r_run, None).astype(BF16), vt)
        r_run = r_run + tot
    run_scr[...] = r_run
    acc_scr[...] = acc

    @pl.when(g == n_pages // ppb - 1)
    def _():
        masked = acc_scr[...] * bd
        o = masked[0:dec, :]
        for h in range(1, SB_HEADS):
            o = o + masked[h * dec:(h + 1) * dec, :]
        o_ref[0] = o.astype(o_ref.dtype)


def _sb_sample(qs, k_rows, v_rows, cache_k, cache_v, layer, page_table, bias, batch):
    n = qs.shape[0]
    dec = n // batch
    n_pages = page_table.shape[1]
    ppb = next(p for p in (16, 8, 4, 2, 1) if n_pages % p == 0)
    rows_n = SB_HEADS * dec
    q3 = qs.reshape(batch, dec, SB_W)

    def as_page(a):
        a = jnp.swapaxes(a.reshape(batch, dec, SB_W), 1, 2)
        return jnp.pad(a, ((0, 0), (0, 0), (0, PAGE - dec)))

    bias_rows = jnp.broadcast_to(jnp.repeat(bias * LOG2E, dec)[:, None], (rows_n, PAGE))
    bd = jnp.asarray((np.arange(rows_n)[:, None] // dec == np.arange(SB_W)[None, :] // HEAD_DIM).astype(np.float32))

    def page_spec(pidx):
        return pl.BlockSpec((1, 1, SB_HEADS, HEAD_DIM, PAGE),
                            lambda b, g, pt: (layer, pt[b, n_pages - 1 - (g * ppb + pidx)], 0, 0, 0))

    const = lambda shape: pl.BlockSpec(shape, lambda b, g, pt: (0,) * len(shape))
    grid_spec = pltpu.PrefetchScalarGridSpec(
        num_scalar_prefetch=1,
        grid=(batch, n_pages // ppb),
        in_specs=[
            pl.BlockSpec((1, dec, SB_W), lambda b, g, pt: (b, 0, 0)),
            pl.BlockSpec((1, SB_W, PAGE), lambda b, g, pt: (b, 0, 0)),
            pl.BlockSpec((1, SB_W, PAGE), lambda b, g, pt: (b, 0, 0)),
            const((rows_n, PAGE)), const((rows_n, SB_W)), const((PAGE, PAGE + LANES)),
        ] + [page_spec(pidx) for pidx in range(ppb)] * 2,
        out_specs=pl.BlockSpec((1, dec, SB_W), lambda b, g, pt: (b, 0, 0)),
        scratch_shapes=[
            pltpu.VMEM((rows_n, SB_W), BF16),
            pltpu.VMEM((rows_n, SB_W), F32),
            pltpu.VMEM((rows_n, PAGE), F32),
        ],
    )
    o = pl.pallas_call(
        functools.partial(_sb_sample_kernel, n_pages=n_pages, ppb=ppb, dec=dec),
        grid_spec=grid_spec,
        out_shape=jax.ShapeDtypeStruct((batch, dec, SB_W), BF16),
        compiler_params=_cparams(("arbitrary", "arbitrary")),
    )(page_table, q3, as_page(k_rows), as_page(v_rows), bias_rows, bd, _cum_matrix(PAGE),
      *([cache_k] * ppb), *([cache_v] * ppb))
    return o.reshape(n, SB_W)


def _outproj_kernel(og_ref, or_ref, os_ref, x_ref, g1_ref, sh_ref, sc_ref, ng_ref, wg_ref, wr_ref, ws_ref,
                    *rest, with_router):
    if with_router:
        router_ref, x1_ref, h2_ref, gates_ref = rest
    else:
        x1_ref, h2_ref = rest
    mix = _dot(og_ref[...], wg_ref[...]) + _dot(or_ref[...], wr_ref[...]) + _dot(os_ref[...], ws_ref[...])
    x1 = x_ref[...] + g1_ref[0] * mix
    x1_ref[...] = x1
    ms = jnp.mean(x1 * x1, axis=-1, keepdims=True)
    h2 = x1 * lax.rsqrt(ms + EPS) * ng_ref[...]
    h2 = h2 * (1.0 + sc_ref[0]) + sh_ref[0]
    h2_ref[...] = h2.astype(BF16)
    if with_router:
        n_exp = 8
        logits = _dot_hi(h2, router_ref[...])
        lane = lax.broadcasted_iota(jnp.int32, logits.shape, 1).astype(F32)
        neg = jnp.float32(-jnp.inf)
        lg = jnp.where(lane < n_exp, logits, neg)
        m1 = jnp.max(lg, axis=1, keepdims=True)
        i1 = jnp.min(jnp.where(lg == m1, lane, float(LANES)), axis=1, keepdims=True)
        lg2 = jnp.where(lane == i1, neg, lg)
        m2 = jnp.max(lg2, axis=1, keepdims=True)
        i2 = jnp.min(jnp.where(lg2 == m2, lane, float(LANES)), axis=1, keepdims=True)
        e2 = jnp.exp(m2 - m1)
        gates_ref[...] = jnp.where(lane == i1, 1.0 / (1.0 + e2), 0.0) + jnp.where(lane == i2, e2 / (1.0 + e2), 0.0)


def _outproj(o_gla, o_rwkv, o_sb, x, mod, norm_g, wg, wr, ws, router):
    n, d = x.shape
    nb = mod.shape[0]
    tm = _tile(n // nb, 512)
    tpb = (n // nb) // tm
    with_router = router is not None
    row = lambda w: pl.BlockSpec((tm, w), lambda i: (i, 0))
    const = lambda shape: pl.BlockSpec(shape, lambda i: (0,) * len(shape))
    in_specs = [row(GLA_V), row(RWKV_W), row(SB_W), row(d),
                _mod_spec(mod, tm, tpb, 2), _mod_spec(mod, tm, tpb, 3), _mod_spec(mod, tm, tpb, 4),
                const((1, d)), const((GLA_V, d)), const((RWKV_W, d)), const((SB_W, d))]
    args = [o_gla, o_rwkv, o_sb, x, mod, mod, mod, norm_g.reshape(1, d), wg, wr, ws]
    out_specs = [row(d), row(d)]
    out_shape = [jax.ShapeDtypeStruct((n, d), F32), jax.ShapeDtypeStruct((n, d), BF16)]
    if with_router:
        in_specs.append(const((d, LANES)))
        args.append(router)
        out_specs.append(row(LANES))
        out_shape.append(jax.ShapeDtypeStruct((n, LANES), F32))
    return pl.pallas_call(
        functools.partial(_outproj_kernel, with_router=with_router),
        grid=(n // tm,),
        in_specs=in_specs, out_specs=out_specs, out_shape=out_shape,
        compiler_params=_cparams(("arbitrary",)),
    )(*args)


def _ffn_kernel(h_ref, x1_ref, g2_ref, *rest, gated):
    if gated:
        gates_ref, w1_ref, w3_ref, w2_ref, o_ref, acc_scr = rest
    else:
        w1_ref, w3_ref, w2_ref, o_ref, acc_scr = rest
    e = pl.program_id(1)
    f = pl.program_id(2)

    @pl.when((e == 0) & (f == 0))
    def _():
        acc_scr[...] = jnp.zeros_like(acc_scr)

    h = h_ref[...]
    a = _dot(h, w1_ref[0])
    t = a * _sigmoid(a) * _dot(h, w3_ref[0])
    if gated:
        gates = gates_ref[...]
        lane = lax.broadcasted_iota(jnp.int32, gates.shape, 1)
        t = t * jnp.sum(jnp.where(lane == e, gates, 0.0), axis=1, keepdims=True)
    acc_scr[...] += _dot(t.astype(BF16), w2_ref[0])

    @pl.when((e == pl.num_programs(1) - 1) & (f == pl.num_programs(2) - 1))
    def _():
        o_ref[...] = x1_ref[...] + g2_ref[0] * acc_scr[...]


def _ffn(h2, x1, mod, w1, w3, w2, gates, tf):
    n, d = x1.shape
    n_exp, _, ff = w1.shape
    nb = mod.shape[0]
    tm = _tile(n // nb, 512)
    tpb = (n // nb) // tm
    gated = gates is not None
    row = lambda w: pl.BlockSpec((tm, w), lambda i, e, f: (i, 0))
    in_specs = [row(d), row(d), _mod_spec(mod, tm, tpb, 5)]
    args = [h2, x1, mod]
    if gated:
        in_specs.append(row(LANES))
        args.append(gates)
    in_specs += [
        pl.BlockSpec((1, d, tf), lambda i, e, f: (e, 0, f)),
        pl.BlockSpec((1, d, tf), lambda i, e, f: (e, 0, f)),
        pl.BlockSpec((1, tf, d), lambda i, e, f: (e, f, 0)),
    ]
    args += [w1, w3, w2]
    return pl.pallas_call(
        functools.partial(_ffn_kernel, gated=gated),
        grid=(n // tm, n_exp, ff // tf),
        in_specs=in_specs,
        out_specs=row(d),
        out_shape=jax.ShapeDtypeStruct((n, d), F32),
        scratch_shapes=[pltpu.VMEM((tm, d), F32)],
        compiler_params=_cparams(("arbitrary", "arbitrary", "arbitrary")),
    )(*args)


def _moe_kernel(cnt_ref, h_ref, x1_ref, g2_ref, gates_ref, pos_ref, w1_ref, w3_ref, w2_ref, o_ref,
                hc_scr, accc_scr, *, tm, ch):
    i = pl.program_id(0)
    e = pl.program_id(1)
    f = pl.program_id(2)
    last_f = f == pl.num_programs(2) - 1
    count = cnt_ref[i, e]
    rank = lax.broadcasted_iota(jnp.int32, (ch, 1), 0)

    def one_hot(c):
        return jnp.where(pos_ref[0, pl.ds(e, 1), :] == rank + c * ch, 1.0, 0.0).astype(BF16)

    @pl.when((e == 0) & (f == 0))
    def _():
        o_ref[...] = jnp.zeros_like(o_ref)

    for c in range(-(-tm // ch)):
        rows = pl.ds(c * ch, ch)

        @pl.when(c * ch < count)
        def _():
            @pl.when(f == 0)
            def _():
                hc_scr[rows, :] = _dot(one_hot(c), h_ref[...]).astype(BF16)
                accc_scr[rows, :] = jnp.zeros((ch, accc_scr.shape[1]), F32)

            hc = hc_scr[rows, :]
            a = _dot(hc, w1_ref[0, 0])
            t = a * _sigmoid(a) * _dot(hc, w3_ref[0, 0])
            accc_scr[rows, :] += _dot(t.astype(BF16), w2_ref[0, 0])

            @pl.when(last_f)
            def _():
                y = accc_scr[rows, :]
                hi = y.astype(BF16)
                lo = (y - hi.astype(F32)).astype(BF16)
                p = one_hot(c)
                gates = gates_ref[...]
                lane = lax.broadcasted_iota(jnp.int32, gates.shape, 1)
                ge = jnp.sum(jnp.where(lane == e, gates, 0.0), axis=1, keepdims=True)
                o_ref[...] += ge * (_dot_tn(p, hi) + _dot_tn(p, lo))

    @pl.when((e == pl.num_programs(1) - 1) & last_f)
    def _():
        o_ref[...] = x1_ref[...] + g2_ref[0] * o_ref[...]


def _moe(h2, x1, mod, w1, w3, w2, gates, tf, layer):
    n, d = x1.shape
    _, n_exp, _, ff = w1.shape
    nb = mod.shape[0]
    tm = _tile(n // nb, 1024)
    tpb = (n // nb) // tm
    ch = -(-int(tm * TOP_K / n_exp * 1.125) // 32) * 32 if tm >= 512 else tm // 2
    slots = -(-tm // ch)
    n_tiles = n // tm
    routed = (gates[:, :n_exp] > 0.0).reshape(n_tiles, tm, n_exp)
    counts = jnp.sum(routed, axis=1).astype(jnp.int32)
    ranks = jnp.cumsum(routed.astype(jnp.int32), axis=1) - 1
    pos = jnp.swapaxes(jnp.where(routed, ranks, -1), 1, 2)
    row = lambda w: pl.BlockSpec((tm, w), lambda i, e, f, cnt: (i, 0))
    grid_spec = pltpu.PrefetchScalarGridSpec(
        num_scalar_prefetch=1,
        grid=(n_tiles, n_exp, ff // tf),
        in_specs=[
            row(d), row(d), _mod_spec(mod, tm, tpb, 5), row(LANES),
            pl.BlockSpec((1, n_exp, tm), lambda i, e, f, cnt: (i, 0, 0)),
            pl.BlockSpec((1, 1, d, tf), lambda i, e, f, cnt: (layer, e, 0, f)),
            pl.BlockSpec((1, 1, d, tf), lambda i, e, f, cnt: (layer, e, 0, f)),
            pl.BlockSpec((1, 1, tf, d), lambda i, e, f, cnt: (layer, e, f, 0)),
        ],
        out_specs=row(d),
        scratch_shapes=[pltpu.VMEM((slots * ch, d), BF16), pltpu.VMEM((slots * ch, d), F32)],
    )
    return pl.pallas_call(
        functools.partial(_moe_kernel, tm=tm, ch=ch),
        grid_spec=grid_spec,
        out_shape=jax.ShapeDtypeStruct((n, d), F32),
        compiler_params=_cparams(("arbitrary", "arbitrary", "arbitrary")),
    )(counts, h2, x1, mod, gates, pos, w1, w3, w2)


def _pad_rows(w, rows, offset):
    return jnp.pad(w, ((0, 0), (offset, rows - offset - w.shape[1]), (0, 0)))


def _layer(l, x, mod, batch, state, W, sb_fn, sb_bias=None):
    n = x.shape[0]
    gla0, rwkv0, shift0 = state
    p_gla, p_rwkv, p_sb = _inproj(x, mod, W['norm1_g'][l], W['w_in'][l])
    chunk = math.gcd(n // batch, GLA_CHUNK)
    o_gla, gla_st = _gla(p_gla, gla0, W['gla_wa2'][l], W['gla_ba'][l], W['gla_ng'][l], batch, chunk)
    o_rwkv, rwkv_st = _rwkv(p_rwkv, shift0, rwkv0, {k: v[l] for k, v in W['rwkv'].items()}, batch, 2)
    qs, k_rows, kb, vb = _sbprep(p_sb, W['sb_qg'][l], W['sb_kg'][l], sb_bias)
    v_rows = p_sb[:, 2 * SB_W:]
    o_sb = sb_fn(qs, kb, vb, k_rows, v_rows)
    router = W['router'][l // 2] if l % 2 else None
    outs = _outproj(o_gla, o_rwkv, o_sb, x, mod, W['norm2_g'][l], W['wo_g'][l], W['wo_r'][l], W['wo_s'][l], router)
    if l % 2 == 0:
        x1, h2 = outs
        x = _ffn(h2, x1, mod, W['ffn_w1'][l // 2][None], W['ffn_w3'][l // 2][None], W['ffn_w2'][l // 2][None],
                 None, W['ffn_w1'].shape[-1] // 2)
    else:
        x1, h2, gates = outs
        x = _moe(h2, x1, mod, W['moe_w1'], W['moe_w3'], W['moe_w2'], gates, W['moe_w1'].shape[-1] // 7, l // 2)
    t = n // batch
    shift_row = p_rwkv.reshape(batch, t, RWKV_COLS)[:, -1]
    return x, (k_rows, v_rows, _gla_state_from_bd(gla_st), _rwkv_state_out(rwkv_st), shift_row)


def kernel(x_prompt, x_sample, cache_sb_k, cache_sb_v, state_gla, state_rwkv, state_shift, page_table, c_prompt, c_sample, norm1_g, norm2_g, w_ada, b_ada, w_in, w_out, gla_w_a2, gla_b_a, gla_norm_g, rwkv_mu, rwkv_w0, rwkv_w_w2, rwkv_a0, rwkv_w_a2, rwkv_w_g2, rwkv_k_k, rwkv_k_a, rwkv_r_k, rwkv_ln_g, rwkv_ln_b, sb_q_g, sb_k_g, sb_bias, ffn_w1, ffn_w3, ffn_w2, moe_router, moe_w1, moe_w3, moe_w2):
    depth = w_in.shape[0]
    bp, seq, d = x_prompt.shape
    bs, dec, _ = x_sample.shape
    n_pool, page = cache_sb_k.shape[1:3]

    g0 = 0
    gq, gk, gv = w_in[:, :, g0:g0 + GLA_QK], w_in[:, :, g0 + GLA_QK:g0 + 2 * GLA_QK], w_in[:, :, g0 + 2 * GLA_QK:g0 + 2 * GLA_QK + GLA_V]
    g_low = w_in[:, :, g0 + 2 * GLA_QK + GLA_V:g0 + 2 * GLA_QK + GLA_V + GLA_LORA]
    g_r = w_in[:, :, g0 + 2 * GLA_QK + GLA_V + GLA_LORA:g0 + 2 * GLA_QK + 2 * GLA_V + GLA_LORA]
    gla_cols = 2 * GLA_QK + 2 * GLA_V + GLA_LORA
    g_low = jnp.pad(g_low, ((0, 0), (0, 0), (0, LANES - GLA_LORA)))
    w_in_r = jnp.concatenate([gq, gk, gv, g_r, g_low, w_in[:, :, gla_cols:]], axis=-1).astype(BF16)

    tile_heads = lambda g, h: jnp.tile(g, (1, h))
    W = dict(
        norm1_g=norm1_g, norm2_g=norm2_g, w_in=w_in_r,
        wo_g=w_out[:, :GLA_V].astype(BF16), wo_r=w_out[:, GLA_V:GLA_V + RWKV_W].astype(BF16),
        wo_s=w_out[:, GLA_V + RWKV_W:].astype(BF16),
        gla_wa2=_pad_rows(gla_w_a2, LANES, 0), gla_ba=gla_b_a[:, None, :],
        gla_ng=tile_heads(gla_norm_g, GLA_HEADS)[:, None, :],
        rwkv=dict(
            mu=rwkv_mu[:, None, :], w0=rwkv_w0[:, None, :], ww2=_pad_rows(rwkv_w_w2, LANES, 0),
            a0=rwkv_a0[:, None, :], wa2=_pad_rows(rwkv_w_a2, LANES, RWKV_W_LORA), wg2=rwkv_w_g2,
            kk=rwkv_k_k[:, None, :], ka=rwkv_k_a[:, None, :], rk=rwkv_r_k.reshape(depth, 1, RWKV_W),
            lng=rwkv_ln_g[:, None, :], lnb=rwkv_ln_b[:, None, :]),
        sb_qg=tile_heads(sb_q_g, SB_HEADS)[:, None, :], sb_kg=tile_heads(sb_k_g, SB_HEADS)[:, None, :],
        ffn_w1=ffn_w1.astype(BF16), ffn_w3=ffn_w3.astype(BF16), ffn_w2=ffn_w2.astype(BF16),
        router=jnp.pad(moe_router, ((0, 0), (0, 0), (0, LANES - moe_router.shape[-1]))),
        moe_w1=moe_w1.astype(BF16), moe_w3=moe_w3.astype(BF16), moe_w2=moe_w2.astype(BF16),
    )

    rows = bp + bs
    rows_pad = -(-rows // 8) * 8
    c_all = jnp.pad(jnp.concatenate([c_prompt, c_sample], axis=0), ((0, rows_pad - rows), (0, 0)))
    mod_all = _modulation(c_all, w_ada, b_ada)

    xp = x_prompt.reshape(bp * seq, d)
    xs = x_sample.reshape(bs * dec, d)
    cache_k = jnp.transpose(cache_sb_k, (0, 1, 3, 4, 2))
    cache_v = jnp.transpose(cache_sb_v, (0, 1, 3, 4, 2))
    zero_p = (jnp.zeros((bp, GLA_V, GLA_QK), F32), jnp.zeros((bp, HEAD_DIM, RWKV_W), F32),
              jnp.zeros((bp, 1, RWKV_COLS), F32))
    st_p, st_s = [], []
    for l in range(depth):
        mod_p = mod_all[l, :bp][:, None, :]
        mod_s = jnp.repeat(mod_all[l, bp:rows], dec, axis=0)[None]
        xp, sp = _layer(l, xp, mod_p, bp, zero_p, W,
                        lambda q, kb, vb, k_rows, v_rows: _sb_prompt(q, kb, vb, bp), sb_bias[l])
        state_s = (_gla_state_to_bd(state_gla[l]), _rwkv_state_in(state_rwkv[l]), state_shift[l][:, None, :])
        xs, ss = _layer(l, xs, mod_s, bs, state_s, W,
                        lambda q, kb, vb, k_rows, v_rows: _sb_sample(q, k_rows, v_rows, cache_k, cache_v, l,
                                                                     page_table, sb_bias[l], bs))
        st_p.append(sp)
        st_s.append(ss)

    def stack(sts, idx, shape):
        return jnp.stack([s[idx] for s in sts]).reshape((depth,) + shape)

    outs = [xp.reshape(bp, seq, d), xs.reshape(bs, dec, d)]
    for sts, b, t in ((st_p, bp, seq), (st_s, bs, dec)):
        outs += [
            stack(sts, 0, (b, t, SB_HEADS, HEAD_DIM)), stack(sts, 1, (b, t, SB_HEADS, HEAD_DIM)),
            stack(sts, 2, (b, GLA_HEADS, GLA_DK, GLA_DV)), stack(sts, 3, (b, RWKV_HEADS, HEAD_DIM, HEAD_DIM)),
            stack(sts, 4, (b, RWKV_COLS)),
        ]
    return tuple(outs)
```

```python
import functools
import math

import numpy as np
import jax
import jax.numpy as jnp
from jax import lax
from jax.experimental import pallas as pl
from jax.experimental.pallas import tpu as pltpu

F32 = jnp.float32
BF16 = jnp.bfloat16
HIGHEST = lax.Precision.HIGHEST

HEAD_DIM = 64
GLA_HEADS = 4
GLA_DK = 32
GLA_DV = 64
GLA_LORA = 16
GLA_TAU = 16.0
GLA_CHUNK = 16
RWKV_HEADS = 4
RWKV_W_LORA = 64
RWKV_A_LORA = 64
RWKV_G_LORA = 128
RWKV_LN_EPS = 64e-5
SB_HEADS = 8
TOP_K = 2
EPS = 1e-6
LOG2E = math.log2(math.e)
SB_DEAD_BITS = 160.0

GLA_QK = GLA_HEADS * GLA_DK
GLA_V = GLA_HEADS * GLA_DV
GLA_PCOLS = 2 * GLA_QK + 2 * GLA_V + 128
RWKV_W = RWKV_HEADS * HEAD_DIM
RWKV_COLS = 3 * RWKV_W + RWKV_W_LORA + RWKV_A_LORA + RWKV_G_LORA
SB_W = SB_HEADS * HEAD_DIM
SB_COLS = 3 * SB_W
LANES = 128
PAGE = 128
VMEM_LIMIT = 56 * 1024 * 1024


def _cparams(sem):
    return pltpu.CompilerParams(dimension_semantics=sem, vmem_limit_bytes=VMEM_LIMIT)


def _tile(n, pref):
    if n <= pref:
        return n
    t = pref
    while n % t:
        t -= 8
    return t


def _block_ones(n, group, value=1.0):
    idx = np.arange(n) // group
    return jnp.asarray((idx[:, None] == idx[None, :]).astype(np.float32) * value)


def _dot(a, b):
    return jnp.dot(a, b, preferred_element_type=F32)


def _dot_hi(a, b):
    return jnp.dot(a, b, preferred_element_type=F32, precision=HIGHEST)


def _dot_nt(a, b, precision=None):
    return lax.dot_general(a, b, (((1,), (1,)), ((), ())), preferred_element_type=F32, precision=precision)


def _dot_tn(a, b, precision=None):
    return lax.dot_general(a, b, (((0,), (0,)), ((), ())), preferred_element_type=F32, precision=precision)


def _dot_split(x, m_bf16):
    hi = x.astype(BF16)
    lo = (x - hi.astype(F32)).astype(BF16)
    return _dot(hi, m_bf16) + _dot(lo, m_bf16)


def _neg_abs(x):
    bits = lax.bitcast_convert_type(x, jnp.uint32) | jnp.uint32(0x80000000)
    return lax.bitcast_convert_type(bits, F32)


def _sigmoid(x):
    return 1.0 / (1.0 + jnp.exp(-x))


def _softplus(x):
    return jnp.maximum(x, 0.0) + jnp.log(1.0 + jnp.exp(-jnp.abs(x)))


def _mod_kernel(c_ref, w_ref, b_ref, o_ref):
    c = c_ref[...]
    o_ref[0] = _dot_hi(c * _sigmoid(c), w_ref[0]) + b_ref[0]


def _modulation(c_all, w_ada, b_ada):
    depth, d, d6 = w_ada.shape
    r = c_all.shape[0]
    tn = 1024
    return pl.pallas_call(
        _mod_kernel,
        grid=(depth, d6 // tn),
        in_specs=[
            pl.BlockSpec((r, d), lambda l, j: (0, 0)),
            pl.BlockSpec((1, d, tn), lambda l, j: (l, 0, j)),
            pl.BlockSpec((1, 1, tn), lambda l, j: (l, 0, j)),
        ],
        out_specs=pl.BlockSpec((1, r, tn), lambda l, j: (l, 0, j)),
        out_shape=jax.ShapeDtypeStruct((depth, r, d6), F32),
        compiler_params=_cparams(("arbitrary", "arbitrary")),
    )(c_all, w_ada, b_ada.reshape(depth, 1, d6))


def _mod_spec(mod, tm, tpb, chunk):
    d = mod.shape[2] // 6
    if mod.shape[1] == 1:
        return pl.BlockSpec((1, 1, d), lambda i, *_: (i // tpb, 0, chunk))
    return pl.BlockSpec((1, tm, d), lambda i, *_: (i // tpb, i % tpb, chunk))


def _inproj_kernel(x_ref, sh_ref, sc_ref, g_ref, w_ref, pg_ref, pr_ref, ps_ref):
    x = x_ref[...]
    ms = jnp.mean(x * x, axis=-1, keepdims=True)
    h = x * lax.rsqrt(ms + EPS) * g_ref[...]
    h = h * (1.0 + sc_ref[0]) + sh_ref[0]
    p = _dot(h.astype(BF16), w_ref[...])
    pg_ref[...] = p[:, :GLA_PCOLS]
    pr_ref[...] = p[:, GLA_PCOLS:GLA_PCOLS + RWKV_COLS]
    ps_ref[...] = p[:, GLA_PCOLS + RWKV_COLS:]


def _inproj(x, mod, norm_g, w):
    n, d = x.shape
    nb = mod.shape[0]
    tm = _tile(n // nb, 256)
    tpb = (n // nb) // tm
    ncols = w.shape[1]
    return pl.pallas_call(
        _inproj_kernel,
        grid=(n // tm,),
        in_specs=[
            pl.BlockSpec((tm, d), lambda i: (i, 0)),
            _mod_spec(mod, tm, tpb, 0),
            _mod_spec(mod, tm, tpb, 1),
            pl.BlockSpec((1, d), lambda i: (0, 0)),
            pl.BlockSpec((d, ncols), lambda i: (0, 0)),
        ],
        out_specs=[
            pl.BlockSpec((tm, GLA_PCOLS), lambda i: (i, 0)),
            pl.BlockSpec((tm, RWKV_COLS), lambda i: (i, 0)),
            pl.BlockSpec((tm, SB_COLS), lambda i: (i, 0)),
        ],
        out_shape=[
            jax.ShapeDtypeStruct((n, GLA_PCOLS), F32),
            jax.ShapeDtypeStruct((n, RWKV_COLS), F32),
            jax.ShapeDtypeStruct((n, SB_COLS), F32),
        ],
        compiler_params=_cparams(("arbitrary",)),
    )(x, mod, mod, norm_g.reshape(1, d), w)


def _gla_kernel(p_ref, s0_ref, wa2_ref, ba_ref, ng_ref, hexp_ref, havg_ref, bd_ref, tril_ref, ones_ref,
                o_ref, s_ref, b_scr, qin_scr, kout_scr, dec_scr, o_scr, *, chunk, n_chunks):
    c_len = chunk

    @pl.when(pl.program_id(1) == 0)
    def _():
        s_ref[0] = s0_ref[0]

    glow = p_ref[0, :, 2 * GLA_QK + 2 * GLA_V:]
    x = _dot_hi(glow, wa2_ref[...]) + ba_ref[...]
    la = -_softplus(-x) * (1.0 / GLA_TAU)
    la_hi = la.astype(BF16)
    la_lo = (la - la_hi.astype(F32)).astype(BF16)
    b = _dot(tril_ref[...], la_hi) + _dot(tril_ref[...], la_lo)
    b_tot = _dot(ones_ref[...], la_hi) + _dot(ones_ref[...], la_lo)
    b_scr[...] = b
    qin_scr[...] = p_ref[0, :, 0:GLA_QK] * (GLA_DK ** -0.5) * jnp.exp(b)
    kout_scr[...] = p_ref[0, :, GLA_QK:2 * GLA_QK] * jnp.exp(b_tot - b)
    dec_scr[...] = jnp.exp(b_tot)
    row = lax.broadcasted_iota(jnp.int32, (c_len, 1), 0)
    hexp = hexp_ref[...]
    bd = bd_ref[...]

    def body(c, carry):
        sl = pl.ds(pl.multiple_of(c * c_len, c_len), c_len)
        q = p_ref[0, sl, 0:GLA_QK] * (GLA_DK ** -0.5)
        k = p_ref[0, sl, GLA_QK:2 * GLA_QK]
        v = p_ref[0, sl, 2 * GLA_QK:2 * GLA_QK + GLA_V]
        b = b_scr[sl, :]
        zs = []
        for s in range(c_len):
            m = row >= s
            e = jnp.exp(jnp.where(m, b - b[s:s + 1, :], 0.0))
            zs.append(jnp.where(m, q * k[s:s + 1, :] * e, 0.0))
        a_exp = _dot_split(jnp.concatenate(zs, axis=0), hexp)
        o = jnp.zeros((c_len, GLA_V), F32)
        for s in range(c_len):
            o = o + a_exp[s * c_len:(s + 1) * c_len, :] * v[s:s + 1, :]
        st = s_ref[0]
        o_scr[sl, :] = o + _dot_nt(qin_scr[sl, :], st, HIGHEST)
        s_ref[0] = st * dec_scr[pl.ds(c * c_len, 1), :] + _dot_tn(v, kout_scr[sl, :], HIGHEST) * bd
        return carry

    lax.fori_loop(0, n_chunks, body, 0, unroll=min(4, n_chunks))
    o = o_scr[...]
    r = p_ref[0, :, 2 * GLA_QK + GLA_V:2 * GLA_QK + 2 * GLA_V]
    on = o * lax.rsqrt(_dot_split(o * o, havg_ref[...]) + EPS) * ng_ref[...]
    o_ref[0] = (on * (r * _sigmoid(r))).astype(o_ref.dtype)


def _gla(p_gla, s0t, wa2p, ba, ng_t, batch, chunk):
    n = p_gla.shape[0]
    t = n // batch
    tt = _tile(t, 512)
    n_chunks = tt // chunk
    p3 = p_gla.reshape(batch, t, GLA_PCOLS)
    hexp = jnp.asarray((np.arange(GLA_QK)[:, None] // GLA_DK == np.arange(GLA_V)[None, :] // GLA_DV).astype(np.float32))
    havg = _block_ones(GLA_V, GLA_DV, 1.0 / GLA_DV)
    bd = jnp.asarray((np.arange(GLA_V)[:, None] // GLA_DV == np.arange(GLA_QK)[None, :] // GLA_DK).astype(np.float32))
    rows = np.arange(tt)
    same = rows[:, None] // chunk == rows[None, :] // chunk
    tril = jnp.asarray((same & (rows[:, None] >= rows[None, :])).astype(np.float32)).astype(BF16)
    ones = jnp.asarray(same.astype(np.float32)).astype(BF16)
    const = lambda shape: pl.BlockSpec(shape, lambda b, i: (0,) * len(shape))
    scr = lambda w: pltpu.VMEM((tt, w), F32)
    o, s = pl.pallas_call(
        functools.partial(_gla_kernel, chunk=chunk, n_chunks=n_chunks),
        grid=(batch, t // tt),
        in_specs=[
            pl.BlockSpec((1, tt, GLA_PCOLS), lambda b, i: (b, i, 0)),
            pl.BlockSpec((1, GLA_V, GLA_QK), lambda b, i: (b, 0, 0)),
            const((LANES, GLA_QK)), const((1, GLA_QK)), const((1, GLA_V)),
            const((GLA_QK, GLA_V)), const((GLA_V, GLA_V)), const((GLA_V, GLA_QK)), const((tt, tt)), const((tt, tt)),
        ],
        out_specs=[
            pl.BlockSpec((1, tt, GLA_V), lambda b, i: (b, i, 0)),
            pl.BlockSpec((1, GLA_V, GLA_QK), lambda b, i: (b, 0, 0)),
        ],
        out_shape=[
            jax.ShapeDtypeStruct((batch, t, GLA_V), BF16),
            jax.ShapeDtypeStruct((batch, GLA_V, GLA_QK), F32),
        ],
        scratch_shapes=[scr(GLA_QK), scr(GLA_QK), scr(GLA_QK), scr(GLA_QK), scr(GLA_V)],
        compiler_params=_cparams(("arbitrary", "arbitrary")),
    )(p3, s0t, wa2p, ba, ng_t, hexp.astype(BF16), havg.astype(BF16), bd, tril, ones)
    return o.reshape(n, GLA_V), s


def _gla_state_to_bd(s0):
    b = s0.shape[0]
    eye = jnp.eye(GLA_HEADS, dtype=s0.dtype)
    st = jnp.einsum('bhkv,hg->bhvgk', s0, eye)
    return st.reshape(b, GLA_V, GLA_QK)


def _gla_state_from_bd(st):
    b = st.shape[0]
    s5 = st.reshape(b, GLA_HEADS, GLA_DV, GLA_HEADS, GLA_DK)
    diag = jnp.stack([s5[:, h, :, h, :] for h in range(GLA_HEADS)], axis=1)
    return jnp.swapaxes(diag, 2, 3)


def _rwkv_kernel(p_ref, sh0_ref, s0_ref, mu_ref, w0_ref, ww2_ref, a0_ref, wa2_ref, wg2_ref, kk_ref, ka_ref,
                 rk_ref, lng_ref, lnb_ref, ones_ref, onesb_ref, avg_ref, sel_ref,
                 o_ref, s_ref, prev_scr, nkk_scr, d_scr, b_scr, k_scr, r_scr, v_scr, y_scr, *, nbb, tt):
    first = pl.program_id(1) == 0

    @pl.when(first)
    def _():
        s_ref[...] = s0_ref[...]
        prev_scr[...] = sh0_ref[...]

    ones_f = ones_ref[...]
    ones_b = onesb_ref[...]
    avg = avg_ref[...]
    sel = sel_ref[...]
    row = lax.broadcasted_iota(jnp.int32, (tt, 1), 0)
    gate = []
    for bb in range(nbb):
        p = p_ref[bb]
        p_prev = jnp.where(row == 0, prev_scr[bb], pltpu.roll(p, 1, axis=0))
        prev_scr[bb] = p[tt - 1:tt, :]
        xm = p + (p_prev - p) * mu_ref[...]
        r = xm[:, 0:RWKV_W]
        k = xm[:, RWKV_W:2 * RWKV_W]
        v = xm[:, 2 * RWKV_W:3 * RWKV_W]
        xwa = xm[:, 3 * RWKV_W:3 * RWKV_W + LANES]
        xg = xm[:, 3 * RWKV_W + LANES:]
        w = -_softplus(-(w0_ref[...] + _dot_hi(jnp.tanh(xwa), ww2_ref[...]))) - 0.5
        a = _sigmoid(a0_ref[...] + _dot_hi(xwa, wa2_ref[...]))
        gate.append(_dot_hi(_sigmoid(xg), wg2_ref[...]))
        kk = k * kk_ref[...]
        nrm = jnp.sqrt(_dot_hi(kk * kk, ones_f))
        kk = kk / jnp.maximum(nrm, 1e-12)
        k2 = k * (1.0 + (a - 1.0) * ka_ref[...])
        nkk_scr[bb] = -kk
        d_scr[bb] = jnp.exp(-jnp.exp(w))
        b_scr[bb] = kk * a
        k_scr[bb] = k2
        r_scr[bb] = r
        v_scr[bb] = v

    def head_sums(xs, split):
        stacked = jnp.concatenate([x[:, half * LANES:(half + 1) * LANES] for x in xs for half in range(2)], axis=0)
        res = _dot_split(stacked, ones_b) if split else _dot(stacked.astype(BF16), ones_b)
        out, row = [], 0
        for x in xs:
            n_rows = x.shape[0]
            out.append(jnp.concatenate([res[row:row + n_rows], res[row + n_rows:row + 2 * n_rows]], axis=1))
            row += 2 * n_rows
        return out

    lane_lo = lax.broadcasted_iota(jnp.int32, (1, LANES), 1) < HEAD_DIM

    def seg_sums(x):
        halves = []
        for half in range(2):
            xh = x[:, half * LANES:(half + 1) * LANES]
            lo = jnp.sum(jnp.where(lane_lo, xh, 0.0), axis=1, keepdims=True)
            hi = jnp.sum(jnp.where(lane_lo, 0.0, xh), axis=1, keepdims=True)
            halves.append(jnp.where(lane_lo, lo, hi))
        return jnp.concatenate(halves, axis=1)

    def read_out(s, bb, ts):
        yb, = head_sums([s * jnp.broadcast_to(r_scr[bb, ts, :], (HEAD_DIM, RWKV_W))], False)
        y_scr[bb, ts, :] = jnp.sum(yb * sel, axis=0, keepdims=True)

    def step(t, states):
        ts = pl.ds(t, 1)
        tp = pl.ds(jnp.maximum(t - 1, 0), 1)
        bc = lambda ref, bb, rows: jnp.broadcast_to(ref[bb, rows, :], (HEAD_DIM, RWKV_W))
        sas = [seg_sums(states[bb] * bc(nkk_scr, bb, ts)) for bb in range(nbb)]
        side = head_sums([jnp.concatenate([states[bb] * bc(r_scr, bb, tp), sel * bc(v_scr, bb, ts)], axis=0)
                          for bb in range(nbb)], False)
        new = []
        for bb in range(nbb):
            y_scr[bb, tp, :] = jnp.sum(side[bb][0:HEAD_DIM] * sel, axis=0, keepdims=True)
            vb = side[bb][HEAD_DIM:]
            new.append(states[bb] * bc(d_scr, bb, ts) + sas[bb] * bc(b_scr, bb, ts) + vb * bc(k_scr, bb, ts))
        return tuple(new)

    states = lax.fori_loop(0, tt, step, tuple(s_ref[bb] for bb in range(nbb)), unroll=2)
    for bb in range(nbb):
        s_ref[bb] = states[bb]
        read_out(states[bb], bb, pl.ds(tt - 1, 1))
        y = y_scr[bb]
        mean = _dot_hi(y, avg)
        yc = y - mean
        var = _dot_hi(yc * yc, avg)
        yn = yc * lax.rsqrt(var + RWKV_LN_EPS) * lng_ref[...] + lnb_ref[...]
        bonus = _dot_hi(r_scr[bb] * k_scr[bb] * rk_ref[...], ones_f) * v_scr[bb]
        o_ref[bb] = ((yn + bonus) * gate[bb]).astype(o_ref.dtype)


def _rwkv(p_rwkv, shift0, s0, wl, batch, nbb):
    n = p_rwkv.shape[0]
    t = n // batch
    tt = _tile(t, 256)
    p3 = p_rwkv.reshape(batch, t, RWKV_COLS)
    ones_f = _block_ones(RWKV_W, HEAD_DIM)
    avg = _block_ones(RWKV_W, HEAD_DIM, 1.0 / HEAD_DIM)
    sel = jnp.asarray((np.arange(HEAD_DIM)[:, None] == np.arange(RWKV_W)[None, :] % HEAD_DIM).astype(np.float32))
    const = lambda shape: pl.BlockSpec(shape, lambda b, i: (0,) * len(shape))
    row = const((1, RWKV_W))
    scr = lambda: pltpu.VMEM((nbb, tt, RWKV_W), F32)
    o, s = pl.pallas_call(
        functools.partial(_rwkv_kernel, nbb=nbb, tt=tt),
        grid=(batch // nbb, t // tt),
        in_specs=[
            pl.BlockSpec((nbb, tt, RWKV_COLS), lambda b, i: (b, i, 0)),
            pl.BlockSpec((nbb, 1, RWKV_COLS), lambda b, i: (b, 0, 0)),
            pl.BlockSpec((nbb, HEAD_DIM, RWKV_W), lambda b, i: (b, 0, 0)),
            const((1, RWKV_COLS)), row, const((LANES, RWKV_W)), row, const((LANES, RWKV_W)),
            const((RWKV_G_LORA, RWKV_W)), row, row, row, row, row,
            const((RWKV_W, RWKV_W)), const((LANES, LANES)), const((RWKV_W, RWKV_W)), const((HEAD_DIM, RWKV_W)),
        ],
        out_specs=[
            pl.BlockSpec((nbb, tt, RWKV_W), lambda b, i: (b, i, 0)),
            pl.BlockSpec((nbb, HEAD_DIM, RWKV_W), lambda b, i: (b, 0, 0)),
        ],
        out_shape=[
            jax.ShapeDtypeStruct((batch, t, RWKV_W), BF16),
            jax.ShapeDtypeStruct((batch, HEAD_DIM, RWKV_W), F32),
        ],
        scratch_shapes=[pltpu.VMEM((nbb, 1, RWKV_COLS), F32)] + [scr() for _ in range(7)],
        compiler_params=_cparams(("arbitrary", "arbitrary")),
    )(p3, shift0, s0, wl['mu'], wl['w0'], wl['ww2'], wl['a0'], wl['wa2'], wl['wg2'], wl['kk'], wl['ka'],
      wl['rk'], wl['lng'], wl['lnb'], ones_f, _block_ones(LANES, HEAD_DIM).astype(BF16), avg, sel)
    return o.reshape(n, RWKV_W), s


def _rwkv_state_in(s0):
    b = s0.shape[0]
    return jnp.transpose(s0, (0, 2, 1, 3)).reshape(b, HEAD_DIM, RWKV_W)


def _rwkv_state_out(s):
    b = s.shape[0]
    return jnp.transpose(s.reshape(b, HEAD_DIM, RWKV_HEADS, HEAD_DIM), (0, 2, 1, 3))


def _sbprep_kernel(p_ref, qg_ref, kg_ref, avg_ref, *rest, aug):
    if aug:
        qrow_ref, krow_ref, q_ref, k_ref, kb_ref, vb_ref = rest
    else:
        q_ref, k_ref, kb_ref, vb_ref = rest
    avg = avg_ref[...]
    q = p_ref[:, 0:SB_W]
    k = p_ref[:, SB_W:2 * SB_W]
    qn = q * lax.rsqrt(_dot_hi(q * q, avg) + EPS) * (qg_ref[...] * (HEAD_DIM ** -0.5 * LOG2E))
    kn = k * lax.rsqrt(_dot_hi(k * k, avg) + EPS) * kg_ref[...]
    k_ref[...] = kn
    vb_ref[...] = p_ref[:, 2 * SB_W:].astype(BF16)
    if aug:
        lo = lax.broadcasted_iota(jnp.int32, (1, LANES), 1) < HEAD_DIM

        def widen(x, row):
            parts = []
            for h in range(SB_HEADS):
                blk = x[:, (h // 2) * LANES:(h // 2 + 1) * LANES]
                if h % 2:
                    blk = pltpu.roll(blk, HEAD_DIM, axis=1)
                parts.append(jnp.where(lo, blk, 0.0))
            return (jnp.concatenate(parts, axis=1) + row).astype(BF16)

        q_ref[...] = widen(qn, qrow_ref[...])
        kb_ref[...] = widen(kn, krow_ref[...])
    else:
        q_ref[...] = qn.astype(BF16)
        kb_ref[...] = kn.astype(BF16)


def _sbprep(p_sb, qg_t, kg_t, bias=None):
    n = p_sb.shape[0]
    tm = _tile(n, 512)
    aug = bias is not None
    avg = _block_ones(SB_W, HEAD_DIM, 1.0 / HEAD_DIM)
    wide = SB_HEADS * LANES if aug else SB_W
    out = lambda w: pl.BlockSpec((tm, w), lambda i: (i, 0))
    const = lambda shape: pl.BlockSpec(shape, lambda i: (0,) * len(shape))
    in_specs = [pl.BlockSpec((tm, SB_COLS), lambda i: (i, 0)), const((1, SB_W)), const((1, SB_W)), const((SB_W, SB_W))]
    args = [p_sb, qg_t, kg_t, avg]
    if aug:
        b2 = bias * LOG2E
        hi = b2.astype(BF16).astype(F32)
        lo = (b2 - hi).astype(BF16).astype(F32)
        qrow = jnp.zeros((SB_HEADS, LANES), F32).at[:, HEAD_DIM].set(hi).at[:, HEAD_DIM + 1].set(lo)
        krow = jnp.zeros((SB_HEADS, LANES), F32).at[:, HEAD_DIM:HEAD_DIM + 2].set(1.0)
        in_specs += [const((1, wide)), const((1, wide))]
        args += [qrow.reshape(1, wide), krow.reshape(1, wide)]
    return pl.pallas_call(
        functools.partial(_sbprep_kernel, aug=aug),
        grid=(n // tm,),
        in_specs=in_specs,
        out_specs=[out(wide), out(SB_W), out(wide), out(SB_W)],
        out_shape=[
            jax.ShapeDtypeStruct((n, wide), BF16),
            jax.ShapeDtypeStruct((n, SB_W), F32),
            jax.ShapeDtypeStruct((n, wide), BF16),
            jax.ShapeDtypeStruct((n, SB_W), BF16),
        ],
        compiler_params=_cparams(("arbitrary",)),
    )(*args)


def _sb_keep(z, mask):
    keep = jnp.maximum(z, 0.0) + jnp.log2(1.0 + jnp.exp2(-jnp.abs(z)))
    if mask is not None:
        keep = jnp.where(mask, keep, 0.0)
    return z - keep, keep.astype(BF16)


def _sb_logits_stage(z, mask, cum):
    zk, kb = _sb_keep(z, mask)
    tk = z.shape[1]
    cs = _dot(kb, cum)
    return zk - cs[:, :tk], cs[:, tk:]


def _sb_weights(part, r_run, mask):
    reps = part.shape[1] // r_run.shape[1]
    e = jnp.exp2(part - (jnp.concatenate([r_run] * reps, axis=1) if reps > 1 else r_run))
    if mask is not None:
        e = jnp.where(mask, e, 0.0)
    return e


def _sb_prompt_kernel(q_ref, k_ref, v_ref, later_ref, o_ref, zk_scr, kb_scr, tot_scr, *, tq):
    i = pl.program_id(2)
    later = later_ref[...]
    rows = lax.broadcasted_iota(jnp.int32, (tq, tq), 0)
    cols = lax.broadcasted_iota(jnp.int32, (tq, tq), 1)
    diag = cols < rows

    def stage1(j, mask):
        ks = pl.ds(pl.multiple_of(j * tq, tq), tq)
        out = []
        for u in range(2):
            z = _dot_nt(q_ref[:, u * LANES:(u + 1) * LANES], k_ref[ks, u * LANES:(u + 1) * LANES])
            keep = jnp.maximum(z, 0.0) + jnp.log2(1.0 + jnp.exp2(_neg_abs(z)))
            if mask is not None:
                keep = jnp.where(mask, keep, 0.0)
            tot = jnp.broadcast_to(jnp.sum(keep, axis=1, keepdims=True), (tq, LANES))
            out.append((z - keep, keep.astype(BF16), tot))
        return out

    def stage2(j, pend, carry, mask, valid):
        vb = v_ref[pl.ds(pl.multiple_of(j * tq, tq), tq), :]
        out = []
        for u in range(2):
            zk, kb, tot = pend[u]
            acc, r_run = carry[u]
            pv = _dot(_sb_weights(zk - _dot(kb, later), r_run, mask).astype(BF16), vb)
            if valid is not None:
                pv = jnp.where(valid, pv, 0.0)
                tot = jnp.where(valid, tot, 0.0)
            out.append((acc + pv, r_run + tot))
        return tuple(out)

    carry = tuple((jnp.zeros((tq, LANES), F32), jnp.zeros((tq, LANES), F32)) for _ in range(2))
    carry = stage2(i, stage1(i, diag), carry, diag, None)
    zk_scr[...] = jnp.zeros_like(zk_scr)
    kb_scr[...] = jnp.zeros_like(kb_scr)
    tot_scr[...] = jnp.zeros_like(tot_scr)

    def half_trip(n, slot, carry):
        nxt = stage1(jnp.maximum(i - 1 - n, 0), None)
        pend = [(zk_scr[slot, u], kb_scr[slot, u], tot_scr[slot, u]) for u in range(2)]
        carry = stage2(jnp.maximum(i - n, 0), pend, carry, None, (n >= 1) & (n <= i))
        for u in range(2):
            zk_scr[1 - slot, u] = nxt[u][0]
            kb_scr[1 - slot, u] = nxt[u][1]
            tot_scr[1 - slot, u] = nxt[u][2]
        return carry

    def body(state):
        m, _, carry = state
        live = jnp.minimum(jnp.min(carry[0][1]), jnp.min(carry[1][1])) < SB_DEAD_BITS
        carry = half_trip(2 * m, 0, carry)
        carry = half_trip(2 * m + 1, 1, carry)
        return m + 1, live, carry

    _, _, carry = lax.while_loop(lambda s: (2 * s[0] < i + 1) & s[1], body, (jnp.int32(0), jnp.bool_(True), carry))
    lo = lax.broadcasted_iota(jnp.int32, (1, LANES), 1) < HEAD_DIM
    o_ref[...] = jnp.where(lo, carry[0][0], carry[1][0]).astype(o_ref.dtype)


def _cum_matrix(tk):
    j = np.arange(tk)
    later = (j[:, None] > j[None, :]).astype(np.float32)
    return jnp.asarray(np.concatenate([later, np.ones((tk, LANES), np.float32)], axis=1)).astype(BF16)


def _sb_prompt(q_aug, k_aug, vb, batch):
    n = vb.shape[0]
    t = n // batch
    tq = _tile(t, 256)
    tpb = t // tq
    return pl.pallas_call(
        functools.partial(_sb_prompt_kernel, tq=tq),
        grid=(batch, SB_HEADS // 2, tpb),
        in_specs=[
            pl.BlockSpec((tq, 2 * LANES), lambda b, h, i: (b * tpb + i, h)),
            pl.BlockSpec((t, 2 * LANES), lambda b, h, i: (b, h)),
            pl.BlockSpec((t, LANES), lambda b, h, i: (b, h)),
            pl.BlockSpec((tq, tq), lambda b, h, i: (0, 0)),
        ],
        out_specs=pl.BlockSpec((tq, LANES), lambda b, h, i: (b * tpb + i, h)),
        out_shape=jax.ShapeDtypeStruct((n, SB_W), BF16),
        scratch_shapes=[pltpu.VMEM((2, 2, tq, tq), F32), pltpu.VMEM((2, 2, tq, tq), BF16),
                        pltpu.VMEM((2, 2, tq, LANES), F32)],
        compiler_params=_cparams(("arbitrary", "arbitrary", "arbitrary")),
    )(q_aug, k_aug, vb, _cum_matrix(tq)[:, :tq])


def _sb_sample_kernel(pt_ref, q_ref, kn_ref, vn_ref, bias_ref, bd_ref, cum_ref, *rest, n_pages, ppb, dec):
    k_refs = rest[:ppb]
    v_refs = rest[ppb:2 * ppb]
    o_ref, qbd_scr, acc_scr, run_scr = rest[2 * ppb:]
    g = pl.program_id(1)
    rows_n = SB_HEADS * dec
    bd = bd_ref[...]
    cum = cum_ref[...]
    bias = bias_ref[...]

    def logits(kt, mask):
        return _sb_logits_stage(_dot(qbd_scr[...], kt.astype(BF16)) + bias, mask, cum)

    def attend(kt, vt, mask):
        part, tot = logits(kt, mask)
        acc_scr[...] += _dot_nt(_sb_weights(part, run_scr[...], mask).astype(BF16), vt.astype(BF16))
        run_scr[...] += tot

    @pl.when(g == 0)
    def _():
        q = q_ref[0].astype(F32)
        qbd_scr[...] = (jnp.concatenate([q] * SB_HEADS, axis=0) * bd).astype(BF16)
        acc_scr[...] = jnp.zeros_like(acc_scr)
        run_scr[...] = jnp.zeros_like(run_scr)
        qi = lax.broadcasted_iota(jnp.int32, (rows_n, PAGE), 0) % dec
        kj = lax.broadcasted_iota(jnp.int32, (rows_n, PAGE), 1)
        attend(kn_ref[0], vn_ref[0], kj < qi)

    pend = [logits(k_refs[pidx][0, 0].reshape(SB_W, PAGE), None) for pidx in range(ppb)]
    r_run = run_scr[...]
    acc = acc_scr[...]
    for pidx in range(ppb):
        part, tot = pend[pidx]
        vt = v_refs[pidx][0, 0].reshape(SB_W, PAGE).astype(BF16)
        acc = acc + _dot_nt(_sb_weights(part, r_run, None).astype(BF16), vt)
        r_run = r_run + tot
    run_scr[...] = r_run
    acc_scr[...] = acc

    @pl.when(g == n_pages // ppb - 1)
    def _():
        masked = acc_scr[...] * bd
        o = masked[0:dec, :]
        for h in range(1, SB_HEADS):
            o = o + masked[h * dec:(h + 1) * dec, :]
        o_ref[0] = o.astype(o_ref.dtype)


def _sb_sample(qs, k_rows, v_rows, cache_k, cache_v, layer, page_table, bias, batch):
    n = qs.shape[0]
    dec = n // batch
    n_pages = page_table.shape[1]
    ppb = next(p for p in (16, 8, 4, 2, 1) if n_pages % p == 0)
    rows_n = SB_HEADS * dec
    q3 = qs.reshape(batch, dec, SB_W)

    def as_page(a):
        a = jnp.swapaxes(a.reshape(batch, dec, SB_W), 1, 2)
        return jnp.pad(a, ((0, 0), (0, 0), (0, PAGE - dec)))

    bias_rows = jnp.broadcast_to(jnp.repeat(bias * LOG2E, dec)[:, None], (rows_n, PAGE))
    bd = jnp.asarray((np.arange(rows_n)[:, None] // dec == np.arange(SB_W)[None, :] // HEAD_DIM).astype(np.float32))

    def page_spec(pidx):
        return pl.BlockSpec((1, 1, SB_HEADS, HEAD_DIM, PAGE),
                            lambda b, g, pt: (layer, pt[b, n_pages - 1 - (g * ppb + pidx)], 0, 0, 0))

    const = lambda shape: pl.BlockSpec(shape, lambda b, g, pt: (0,) * len(shape))
    grid_spec = pltpu.PrefetchScalarGridSpec(
        num_scalar_prefetch=1,
        grid=(batch, n_pages // ppb),
        in_specs=[
            pl.BlockSpec((1, dec, SB_W), lambda b, g, pt: (b, 0, 0)),
            pl.BlockSpec((1, SB_W, PAGE), lambda b, g, pt: (b, 0, 0)),
            pl.BlockSpec((1, SB_W, PAGE), lambda b, g, pt: (b, 0, 0)),
            const((rows_n, PAGE)), const((rows_n, SB_W)), const((PAGE, PAGE + LANES)),
        ] + [page_spec(pidx) for pidx in range(ppb)] * 2,
        out_specs=pl.BlockSpec((1, dec, SB_W), lambda b, g, pt: (b, 0, 0)),
        scratch_shapes=[
            pltpu.VMEM((rows_n, SB_W), BF16),
            pltpu.VMEM((rows_n, SB_W), F32),
            pltpu.VMEM((rows_n, PAGE), F32),
        ],
    )
    o = pl.pallas_call(
        functools.partial(_sb_sample_kernel, n_pages=n_pages, ppb=ppb, dec=dec),
        grid_spec=grid_spec,
        out_shape=jax.ShapeDtypeStruct((batch, dec, SB_W), BF16),
        compiler_params=_cparams(("arbitrary", "arbitrary")),
    )(page_table, q3, as_page(k_rows), as_page(v_rows), bias_rows, bd, _cum_matrix(PAGE),
      *([cache_k] * ppb), *([cache_v] * ppb))
    return o.reshape(n, SB_W)


def _outproj_kernel(og_ref, or_ref, os_ref, x_ref, g1_ref, sh_ref, sc_ref, ng_ref, wg_ref, wr_ref, ws_ref,
                    *rest, with_router):
    if with_router:
        router_ref, x1_ref, h2_ref, gates_ref = rest
    else:
        x1_ref, h2_ref = rest
    mix = _dot(og_ref[...], wg_ref[...]) + _dot(or_ref[...], wr_ref[...]) + _dot(os_ref[...], ws_ref[...])
    x1 = x_ref[...] + g1_ref[0] * mix
    x1_ref[...] = x1
    ms = jnp.mean(x1 * x1, axis=-1, keepdims=True)
    h2 = x1 * lax.rsqrt(ms + EPS) * ng_ref[...]
    h2 = h2 * (1.0 + sc_ref[0]) + sh_ref[0]
    h2_ref[...] = h2.astype(BF16)
    if with_router:
        n_exp = 8
        logits = _dot_hi(h2, router_ref[...])
        lane = lax.broadcasted_iota(jnp.int32, logits.shape, 1).astype(F32)
        neg = jnp.float32(-jnp.inf)
        lg = jnp.where(lane < n_exp, logits, neg)
        m1 = jnp.max(lg, axis=1, keepdims=True)
        i1 = jnp.min(jnp.where(lg == m1, lane, float(LANES)), axis=1, keepdims=True)
        lg2 = jnp.where(lane == i1, neg, lg)
        m2 = jnp.max(lg2, axis=1, keepdims=True)
        i2 = jnp.min(jnp.where(lg2 == m2, lane, float(LANES)), axis=1, keepdims=True)
        e2 = jnp.exp(m2 - m1)
        gates_ref[...] = jnp.where(lane == i1, 1.0 / (1.0 + e2), 0.0) + jnp.where(lane == i2, e2 / (1.0 + e2), 0.0)


def _outproj(o_gla, o_rwkv, o_sb, x, mod, norm_g, wg, wr, ws, router):
    n, d = x.shape
    nb = mod.shape[0]
    tm = _tile(n // nb, 512)
    tpb = (n // nb) // tm
    with_router = router is not None
    row = lambda w: pl.BlockSpec((tm, w), lambda i: (i, 0))
    const = lambda shape: pl.BlockSpec(shape, lambda i: (0,) * len(shape))
    in_specs = [row(GLA_V), row(RWKV_W), row(SB_W), row(d),
                _mod_spec(mod, tm, tpb, 2), _mod_spec(mod, tm, tpb, 3), _mod_spec(mod, tm, tpb, 4),
                const((1, d)), const((GLA_V, d)), const((RWKV_W, d)), const((SB_W, d))]
    args = [o_gla, o_rwkv, o_sb, x, mod, mod, mod, norm_g.reshape(1, d), wg, wr, ws]
    out_specs = [row(d), row(d)]
    out_shape = [jax.ShapeDtypeStruct((n, d), F32), jax.ShapeDtypeStruct((n, d), BF16)]
    if with_router:
        in_specs.append(const((d, LANES)))
        args.append(router)
        out_specs.append(row(LANES))
        out_shape.append(jax.ShapeDtypeStruct((n, LANES), F32))
    return pl.pallas_call(
        functools.partial(_outproj_kernel, with_router=with_router),
        grid=(n // tm,),
        in_specs=in_specs, out_specs=out_specs, out_shape=out_shape,
        compiler_params=_cparams(("arbitrary",)),
    )(*args)


def _ffn_kernel(h_ref, x1_ref, g2_ref, *rest, gated):
    if gated:
        gates_ref, w1_ref, w3_ref, w2_ref, o_ref, acc_scr = rest
    else:
        w1_ref, w3_ref, w2_ref, o_ref, acc_scr = rest
    e = pl.program_id(1)
    f = pl.program_id(2)

    @pl.when((e == 0) & (f == 0))
    def _():
        acc_scr[...] = jnp.zeros_like(acc_scr)

    h = h_ref[...]
    a = _dot(h, w1_ref[0])
    t = a * _sigmoid(a) * _dot(h, w3_ref[0])
    if gated:
        gates = gates_ref[...]
        lane = lax.broadcasted_iota(jnp.int32, gates.shape, 1)
        t = t * jnp.sum(jnp.where(lane == e, gates, 0.0), axis=1, keepdims=True)
    acc_scr[...] += _dot(t.astype(BF16), w2_ref[0])

    @pl.when((e == pl.num_programs(1) - 1) & (f == pl.num_programs(2) - 1))
    def _():
        o_ref[...] = x1_ref[...] + g2_ref[0] * acc_scr[...]


def _ffn(h2, x1, mod, w1, w3, w2, gates, tf):
    n, d = x1.shape
    n_exp, _, ff = w1.shape
    nb = mod.shape[0]
    tm = _tile(n // nb, 512)
    tpb = (n // nb) // tm
    gated = gates is not None
    row = lambda w: pl.BlockSpec((tm, w), lambda i, e, f: (i, 0))
    in_specs = [row(d), row(d), _mod_spec(mod, tm, tpb, 5)]
    args = [h2, x1, mod]
    if gated:
        in_specs.append(row(LANES))
        args.append(gates)
    in_specs += [
        pl.BlockSpec((1, d, tf), lambda i, e, f: (e, 0, f)),
        pl.BlockSpec((1, d, tf), lambda i, e, f: (e, 0, f)),
        pl.BlockSpec((1, tf, d), lambda i, e, f: (e, f, 0)),
    ]
    args += [w1, w3, w2]
    return pl.pallas_call(
        functools.partial(_ffn_kernel, gated=gated),
        grid=(n // tm, n_exp, ff // tf),
        in_specs=in_specs,
        out_specs=row(d),
        out_shape=jax.ShapeDtypeStruct((n, d), F32),
        scratch_shapes=[pltpu.VMEM((tm, d), F32)],
        compiler_params=_cparams(("arbitrary", "arbitrary", "arbitrary")),
    )(*args)


def _moe_kernel(cnt_ref, h_ref, x1_ref, g2_ref, gates_ref, pos_ref, w1_ref, w3_ref, w2_ref, o_ref,
                hc_scr, accc_scr, *, tm, ch):
    i = pl.program_id(0)
    e = pl.program_id(1)
    f = pl.program_id(2)
    last_f = f == pl.num_programs(2) - 1
    count = cnt_ref[i, e]
    rank = lax.broadcasted_iota(jnp.int32, (ch, 1), 0)

    def one_hot(c):
        return jnp.where(pos_ref[0, pl.ds(e, 1), :] == rank + c * ch, 1.0, 0.0).astype(BF16)

    @pl.when((e == 0) & (f == 0))
    def _():
        o_ref[...] = jnp.zeros_like(o_ref)

    for c in range(-(-tm // ch)):
        rows = pl.ds(c * ch, ch)

        @pl.when(c * ch < count)
        def _():
            @pl.when(f == 0)
            def _():
                hc_scr[rows, :] = _dot(one_hot(c), h_ref[...]).astype(BF16)
                accc_scr[rows, :] = jnp.zeros((ch, accc_scr.shape[1]), F32)

            hc = hc_scr[rows, :]
            a = _dot(hc, w1_ref[0, 0])
            t = a * _sigmoid(a) * _dot(hc, w3_ref[0, 0])
            accc_scr[rows, :] += _dot(t.astype(BF16), w2_ref[0, 0])

            @pl.when(last_f)
            def _():
                y = accc_scr[rows, :]
                hi = y.astype(BF16)
                lo = (y - hi.astype(F32)).astype(BF16)
                p = one_hot(c)
                gates = gates_ref[...]
                lane = lax.broadcasted_iota(jnp.int32, gates.shape, 1)
                ge = jnp.sum(jnp.where(lane == e, gates, 0.0), axis=1, keepdims=True)
                o_ref[...] += ge * (_dot_tn(p, hi) + _dot_tn(p, lo))

    @pl.when((e == pl.num_programs(1) - 1) & last_f)
    def _():
        o_ref[...] = x1_ref[...] + g2_ref[0] * o_ref[...]


def _moe(h2, x1, mod, w1, w3, w2, gates, tf, layer):
    n, d = x1.shape
    _, n_exp, _, ff = w1.shape
    nb = mod.shape[0]
    tm = _tile(n // nb, 1024)
    tpb = (n // nb) // tm
    ch = -(-int(tm * TOP_K / n_exp * 1.125) // 32) * 32 if tm >= 512 else tm // 2
    slots = -(-tm // ch)
    n_tiles = n // tm
    routed = (gates[:, :n_exp] > 0.0).reshape(n_tiles, tm, n_exp)
    counts = jnp.sum(routed, axis=1).astype(jnp.int32)
    ranks = jnp.cumsum(routed.astype(jnp.int32), axis=1) - 1
    pos = jnp.swapaxes(jnp.where(routed, ranks, -1), 1, 2)
    row = lambda w: pl.BlockSpec((tm, w), lambda i, e, f, cnt: (i, 0))
    grid_spec = pltpu.PrefetchScalarGridSpec(
        num_scalar_prefetch=1,
        grid=(n_tiles, n_exp, ff // tf),
        in_specs=[
            row(d), row(d), _mod_spec(mod, tm, tpb, 5), row(LANES),
            pl.BlockSpec((1, n_exp, tm), lambda i, e, f, cnt: (i, 0, 0)),
            pl.BlockSpec((1, 1, d, tf), lambda i, e, f, cnt: (layer, e, 0, f)),
            pl.BlockSpec((1, 1, d, tf), lambda i, e, f, cnt: (layer, e, 0, f)),
            pl.BlockSpec((1, 1, tf, d), lambda i, e, f, cnt: (layer, e, f, 0)),
        ],
        out_specs=row(d),
        scratch_shapes=[pltpu.VMEM((slots * ch, d), BF16), pltpu.VMEM((slots * ch, d), F32)],
    )
    return pl.pallas_call(
        functools.partial(_moe_kernel, tm=tm, ch=ch),
        grid_spec=grid_spec,
        out_shape=jax.ShapeDtypeStruct((n, d), F32),
        compiler_params=_cparams(("arbitrary", "arbitrary", "arbitrary")),
    )(counts, h2, x1, mod, gates, pos, w1, w3, w2)


def _pad_rows(w, rows, offset):
    return jnp.pad(w, ((0, 0), (offset, rows - offset - w.shape[1]), (0, 0)))


def _layer(l, x, mod, batch, state, W, sb_fn, sb_bias=None):
    n = x.shape[0]
    gla0, rwkv0, shift0 = state
    p_gla, p_rwkv, p_sb = _inproj(x, mod, W['norm1_g'][l], W['w_in'][l])
    chunk = math.gcd(n // batch, GLA_CHUNK)
    o_gla, gla_st = _gla(p_gla, gla0, W['gla_wa2'][l], W['gla_ba'][l], W['gla_ng'][l], batch, chunk)
    o_rwkv, rwkv_st = _rwkv(p_rwkv, shift0, rwkv0, {k: v[l] for k, v in W['rwkv'].items()}, batch, 2)
    qs, k_rows, kb, vb = _sbprep(p_sb, W['sb_qg'][l], W['sb_kg'][l], sb_bias)
    v_rows = p_sb[:, 2 * SB_W:]
    o_sb = sb_fn(qs, kb, vb, k_rows, v_rows)
    router = W['router'][l // 2] if l % 2 else None
    outs = _outproj(o_gla, o_rwkv, o_sb, x, mod, W['norm2_g'][l], W['wo_g'][l], W['wo_r'][l], W['wo_s'][l], router)
    if l % 2 == 0:
        x1, h2 = outs
        x = _ffn(h2, x1, mod, W['ffn_w1'][l // 2][None], W['ffn_w3'][l // 2][None], W['ffn_w2'][l // 2][None],
                 None, W['ffn_w1'].shape[-1] // 2)
    else:
        x1, h2, gates = outs
        x = _moe(h2, x1, mod, W['moe_w1'], W['moe_w3'], W['moe_w2'], gates, W['moe_w1'].shape[-1] // 4, l // 2)
    t = n // batch
    shift_row = p_rwkv.reshape(batch, t, RWKV_COLS)[:, -1]
    return x, (k_rows, v_rows, _gla_state_from_bd(gla_st), _rwkv_state_out(rwkv_st), shift_row)


def kernel(x_prompt, x_sample, cache_sb_k, cache_sb_v, state_gla, state_rwkv, state_shift, page_table, c_prompt, c_sample, norm1_g, norm2_g, w_ada, b_ada, w_in, w_out, gla_w_a2, gla_b_a, gla_norm_g, rwkv_mu, rwkv_w0, rwkv_w_w2, rwkv_a0, rwkv_w_a2, rwkv_w_g2, rwkv_k_k, rwkv_k_a, rwkv_r_k, rwkv_ln_g, rwkv_ln_b, sb_q_g, sb_k_g, sb_bias, ffn_w1, ffn_w3, ffn_w2, moe_router, moe_w1, moe_w3, moe_w2):
    depth = w_in.shape[0]
    bp, seq, d = x_prompt.shape
    bs, dec, _ = x_sample.shape
    n_pool, page = cache_sb_k.shape[1:3]

    g0 = 0
    gq, gk, gv = w_in[:, :, g0:g0 + GLA_QK], w_in[:, :, g0 + GLA_QK:g0 + 2 * GLA_QK], w_in[:, :, g0 + 2 * GLA_QK:g0 + 2 * GLA_QK + GLA_V]
    g_low = w_in[:, :, g0 + 2 * GLA_QK + GLA_V:g0 + 2 * GLA_QK + GLA_V + GLA_LORA]
    g_r = w_in[:, :, g0 + 2 * GLA_QK + GLA_V + GLA_LORA:g0 + 2 * GLA_QK + 2 * GLA_V + GLA_LORA]
    gla_cols = 2 * GLA_QK + 2 * GLA_V + GLA_LORA
    g_low = jnp.pad(g_low, ((0, 0), (0, 0), (0, LANES - GLA_LORA)))
    w_in_r = jnp.concatenate([gq, gk, gv, g_r, g_low, w_in[:, :, gla_cols:]], axis=-1).astype(BF16)

    tile_heads = lambda g, h: jnp.tile(g, (1, h))
    W = dict(
        norm1_g=norm1_g, norm2_g=norm2_g, w_in=w_in_r,
        wo_g=w_out[:, :GLA_V].astype(BF16), wo_r=w_out[:, GLA_V:GLA_V + RWKV_W].astype(BF16),
        wo_s=w_out[:, GLA_V + RWKV_W:].astype(BF16),
        gla_wa2=_pad_rows(gla_w_a2, LANES, 0), gla_ba=gla_b_a[:, None, :],
        gla_ng=tile_heads(gla_norm_g, GLA_HEADS)[:, None, :],
        rwkv=dict(
            mu=rwkv_mu[:, None, :], w0=rwkv_w0[:, None, :], ww2=_pad_rows(rwkv_w_w2, LANES, 0),
            a0=rwkv_a0[:, None, :], wa2=_pad_rows(rwkv_w_a2, LANES, RWKV_W_LORA), wg2=rwkv_w_g2,
            kk=rwkv_k_k[:, None, :], ka=rwkv_k_a[:, None, :], rk=rwkv_r_k.reshape(depth, 1, RWKV_W),
            lng=rwkv_ln_g[:, None, :], lnb=rwkv_ln_b[:, None, :]),
        sb_qg=tile_heads(sb_q_g, SB_HEADS)[:, None, :], sb_kg=tile_heads(sb_k_g, SB_HEADS)[:, None, :],
        ffn_w1=ffn_w1.astype(BF16), ffn_w3=ffn_w3.astype(BF16), ffn_w2=ffn_w2.astype(BF16),
        router=jnp.pad(moe_router, ((0, 0), (0, 0), (0, LANES - moe_router.shape[-1]))),
        moe_w1=moe_w1.astype(BF16), moe_w3=moe_w3.astype(BF16), moe_w2=moe_w2.astype(BF16),
    )

    rows = bp + bs
    rows_pad = -(-rows // 8) * 8
    c_all = jnp.pad(jnp.concatenate([c_prompt, c_sample], axis=0), ((0, rows_pad - rows), (0, 0)))
    mod_all = _modulation(c_all, w_ada, b_ada)

    xp = x_prompt.reshape(bp * seq, d)
    xs = x_sample.reshape(bs * dec, d)
    cache_k = jnp.transpose(cache_sb_k, (0, 1, 3, 4, 2))
    cache_v = jnp.transpose(cache_sb_v, (0, 1, 3, 4, 2))
    zero_p = (jnp.zeros((bp, GLA_V, GLA_QK), F32), jnp.zeros((bp, HEAD_DIM, RWKV_W), F32),
              jnp.zeros((bp, 1, RWKV_COLS), F32))
    st_p, st_s = [], []
    for l in range(depth):
        mod_p = mod_all[l, :bp][:, None, :]
        mod_s = jnp.repeat(mod_all[l, bp:rows], dec, axis=0)[None]
        xp, sp = _layer(l, xp, mod_p, bp, zero_p, W,
                        lambda q, kb, vb, k_rows, v_rows: _sb_prompt(q, kb, vb, bp), sb_bias[l])
        state_s = (_gla_state_to_bd(state_gla[l]), _rwkv_state_in(state_rwkv[l]), state_shift[l][:, None, :])
        xs, ss = _layer(l, xs, mod_s, bs, state_s, W,
                        lambda q, kb, vb, k_rows, v_rows: _sb_sample(q, k_rows, v_rows, cache_k, cache_v, l,
                                                                     page_table, sb_bias[l], bs))
        st_p.append(sp)
        st_s.append(ss)

    def stack(sts, idx, shape):
        return jnp.stack([s[idx] for s in sts]).reshape((depth,) + shape)

    outs = [xp.reshape(bp, seq, d), xs.reshape(bs, dec, d)]
    for sts, b, t in ((st_p, bp, seq), (st_s, bs, dec)):
        outs += [
            stack(sts, 0, (b, t, SB_HEADS, HEAD_DIM)), stack(sts, 1, (b, t, SB_HEADS, HEAD_DIM)),
            stack(sts, 2, (b, GLA_HEADS, GLA_DK, GLA_DV)), stack(sts, 3, (b, RWKV_HEADS, HEAD_DIM, HEAD_DIM)),
            stack(sts, 4, (b, RWKV_COLS)),
        ]
    return tuple(outs)
```
